```python
import jax, jax.numpy as jnp
from jax import lax
import numpy as np

D_MODEL = 4096
BATCH = 2
SEQ = 8192
DEPTH = 1

HEAD_DIM = 128
FOX_HEADS = 16
SB_HEADS = 16
FOX_WIDTH = FOX_HEADS * HEAD_DIM
SB_WIDTH = SB_HEADS * HEAD_DIM
Q_BLOCK = 128
N_EXPERTS = 64
TOP_K = 8
N_GROUPS = 8
TOPK_GROUPS = 4
EXPERT_DIM = 512
SHARED_DIM = 512
ROUTED_SCALE = 2.5
EXPERT_BLOCK = 128
LN_EPS = 1e-5
DEEPNORM_ALPHA = (2 * DEPTH) ** 0.25
DEEPNORM_BETA = (8 * DEPTH) ** -0.25
IN_WIDTH = 3 * FOX_WIDTH + FOX_HEADS + 3 * SB_WIDTH + 2 * D_MODEL
PROJ_SPLITS = (FOX_WIDTH, 2 * FOX_WIDTH, 3 * FOX_WIDTH, 3 * FOX_WIDTH + FOX_HEADS,
               3 * FOX_WIDTH + FOX_HEADS + SB_WIDTH,
               3 * FOX_WIDTH + FOX_HEADS + 2 * SB_WIDTH,
               3 * FOX_WIDTH + FOX_HEADS + 3 * SB_WIDTH)

kernel_name = "fox_stickbreak_gated_moe_deepnorm_adaln"


def _layer_norm(x, g, b):
    xf = x.astype(jnp.float32)
    mu = jnp.mean(xf, axis=-1, keepdims=True)
    var = jnp.mean(jnp.square(xf - mu), axis=-1, keepdims=True)
    y = (xf - mu) * lax.rsqrt(var + LN_EPS) * g.astype(jnp.float32) + b.astype(jnp.float32)
    return y.astype(x.dtype)


def _to_heads(a, n_heads):
    b, s, _ = a.shape
    return a.reshape(b, s, n_heads, HEAD_DIM).transpose(0, 2, 1, 3)


def _from_heads(a):
    b, h, s, d = a.shape
    return a.transpose(0, 2, 1, 3).reshape(b, s, h * d)


def _query_blocks(a):
    b, h, s = a.shape[:3]
    a = a.reshape((b, h, s // Q_BLOCK, Q_BLOCK) + a.shape[3:])
    return jnp.moveaxis(a, 2, 0)


def _merge_query_blocks(o):
    nq, b, h, q, d = o.shape
    return jnp.moveaxis(o, 0, 2).reshape(b, h, nq * q, d)


def _forgetting_attention(q, k, v, log_f):
    s_len = q.shape[2]
    nq = s_len // Q_BLOCK
    scale = HEAD_DIM ** -0.5
    cum = jnp.cumsum(log_f, axis=-1)
    kpos = jnp.arange(s_len)

    def one_block(args):
        qb, cum_q, i = args
        qpos = i * Q_BLOCK + jnp.arange(Q_BLOCK)
        logits = jnp.einsum('bhqd,bhkd->bhqk', qb, k, preferred_element_type=jnp.float32) * scale
        logits = logits + cum_q[..., :, None] - cum[..., None, :]
        logits = jnp.where(kpos[None, :] <= qpos[:, None], logits, -jnp.inf)
        p = jax.nn.softmax(logits, axis=-1)
        return jnp.einsum('bhqk,bhkd->bhqd', p.astype(v.dtype), v)

    out = lax.map(one_block, (_query_blocks(q), _query_blocks(cum), jnp.arange(nq)))
    return _merge_query_blocks(out)


def _stick_breaking_attention(q, k, v):
    s_len = q.shape[2]
    nq = s_len // Q_BLOCK
    scale = HEAD_DIM ** -0.5
    kpos = jnp.arange(s_len)

    def one_block(args):
        qb, i = args
        qpos = i * Q_BLOCK + jnp.arange(Q_BLOCK)
        z = jnp.einsum('bhqd,bhkd->bhqk', qb, k, preferred_element_type=jnp.float32) * scale
        valid = kpos[None, :] < qpos[:, None]
        neg_log_1m_beta = jnp.where(valid, jax.nn.softplus(z), 0.0)
        tail = lax.cumsum(neg_log_1m_beta, axis=3, reverse=True) - neg_log_1m_beta
        weights = jnp.where(valid, jnp.exp(jax.nn.log_sigmoid(z) - tail), 0.0)
        return jnp.einsum('bhqk,bhkd->bhqd', weights.astype(v.dtype), v)

    out = lax.map(one_block, (_query_blocks(q), jnp.arange(nq)))
    return _merge_query_blocks(out)


def _token_mixer(h, w_in, b_forget, b_gate, w_branch_fox, w_branch_sb, w_out):
    proj = h @ w_in
    qa, ka, va, fa, qb, kb, vb, gates = jnp.split(proj, PROJ_SPLITS, axis=-1)
    log_f = jax.nn.log_sigmoid(fa.astype(jnp.float32) + b_forget.astype(jnp.float32))
    log_f = log_f.transpose(0, 2, 1)
    o_fox = _forgetting_attention(_to_heads(qa, FOX_HEADS), _to_heads(ka, FOX_HEADS),
                                  _to_heads(va, FOX_HEADS), log_f)
    o_sb = _stick_breaking_attention(_to_heads(qb, SB_HEADS), _to_heads(kb, SB_HEADS),
                                     _to_heads(vb, SB_HEADS))
    g = jax.nn.sigmoid(gates.astype(jnp.float32) + b_gate.astype(jnp.float32))
    g_fox, g_sb = jnp.split(g, 2, axis=-1)
    y_fox = (_from_heads(o_fox) @ w_branch_fox).astype(jnp.float32)
    y_sb = (_from_heads(o_sb) @ w_branch_sb).astype(jnp.float32)
    merged = (g_fox * y_fox + g_sb * y_sb).astype(h.dtype)
    return merged @ w_out


def _swiglu(t, w_g, w_u, w_d):
    return (jax.nn.silu(t @ w_g) * (t @ w_u)) @ w_d


def _moe(h, w_router, router_bias, w_exp_gate, w_exp_up, w_exp_down, w_sh_gate, w_sh_up, w_sh_down):
    b, s, d = h.shape
    n_tok = b * s
    t = h.reshape(n_tok, d)
    scores = jax.nn.sigmoid((t @ w_router).astype(jnp.float32))
    s_choice = scores + router_bias.astype(jnp.float32)
    grouped = s_choice.reshape(n_tok, N_GROUPS, N_EXPERTS // N_GROUPS)
    group_score = jnp.sum(lax.top_k(grouped, 2)[0], axis=-1)
    _, group_idx = lax.top_k(group_score, TOPK_GROUPS)
    group_mask = jnp.any(group_idx[..., None] == jnp.arange(N_GROUPS), axis=-2)
    expert_mask = jnp.repeat(group_mask, N_EXPERTS // N_GROUPS, axis=-1)
    _, top_idx = lax.top_k(jnp.where(expert_mask, s_choice, -jnp.inf), TOP_K)
    top_w = jnp.take_along_axis(scores, top_idx, axis=-1)
    top_w = top_w / (jnp.sum(top_w, axis=-1, keepdims=True) + 1e-20) * ROUTED_SCALE

    n_assign = n_tok * TOP_K
    flat_e = top_idx.reshape(-1).astype(jnp.int32)
    flat_w = top_w.reshape(-1)
    flat_tok = jnp.repeat(jnp.arange(n_tok, dtype=jnp.int32), TOP_K)
    order = jnp.argsort(flat_e)
    sorted_e = flat_e[order]
    counts = jnp.bincount(flat_e, length=N_EXPERTS).astype(jnp.int32)
    padded = (counts + EXPERT_BLOCK - 1) // EXPERT_BLOCK * EXPERT_BLOCK
    start_unpad = jnp.cumsum(counts) - counts
    end_pad = jnp.cumsum(padded)
    start_pad = end_pad - padded
    rank = jnp.arange(n_assign, dtype=jnp.int32) - start_unpad[sorted_e]
    dest = start_pad[sorted_e] + rank
    n_rows = (-(-n_assign // EXPERT_BLOCK)) * EXPERT_BLOCK + N_EXPERTS * EXPERT_BLOCK
    n_blocks = n_rows // EXPERT_BLOCK
    tok_buf = jnp.zeros((n_rows,), jnp.int32).at[dest].set(flat_tok[order])
    w_buf = jnp.zeros((n_rows,), jnp.float32).at[dest].set(flat_w[order])
    block_e = jnp.minimum(
        jnp.searchsorted(end_pad, jnp.arange(n_blocks, dtype=jnp.int32) * EXPERT_BLOCK, side='right'),
        N_EXPERTS - 1)

    def one_block(args):
        tok, wt, e = args
        y = _swiglu(t[tok], w_exp_gate[e], w_exp_up[e], w_exp_down[e])
        return y * wt[:, None].astype(y.dtype)

    ys = lax.map(one_block, (tok_buf.reshape(n_blocks, EXPERT_BLOCK),
                             w_buf.reshape(n_blocks, EXPERT_BLOCK), block_e))
    routed = jnp.zeros((n_tok, d), jnp.float32).at[tok_buf].add(
        ys.reshape(n_rows, d).astype(jnp.float32))
    shared = _swiglu(t, w_sh_gate, w_sh_up, w_sh_down)
    return (routed.astype(h.dtype) + shared).reshape(b, s, d)


def setup_inputs(seed: int = 0) -> dict:
    key = jax.random.key(seed)
    ks = jax.random.split(key, 24)
    f32 = jnp.float32
    D, L = D_MODEL, DEPTH

    def nrm(k, shape, scale):
        return jax.random.normal(k, shape, f32) * scale

    col_scale = np.concatenate([
        np.ones(2 * FOX_WIDTH), np.full(FOX_WIDTH, DEEPNORM_BETA), np.ones(FOX_HEADS),
        np.ones(2 * SB_WIDTH), np.full(SB_WIDTH, DEEPNORM_BETA), np.ones(2 * D)]).astype(np.float32)
    b_forget = (jnp.broadcast_to(jnp.linspace(1.0, 6.0, FOX_HEADS, dtype=f32), (L, FOX_HEADS))
                + nrm(ks[4], (L, FOX_HEADS), 0.1))
    return {
        "x": nrm(ks[0], (BATCH, SEQ, D), 1.0),
        "c": nrm(ks[1], (BATCH, D), 1.0),
        "w_ada": nrm(ks[2], (L, D, 6 * D), D ** -0.5),
        "b_ada": nrm(ks[3], (L, 6 * D), 0.01),
        "w_in": nrm(ks[5], (L, D, IN_WIDTH), D ** -0.5) * jnp.asarray(col_scale),
        "b_forget": b_forget,
        "b_gate": nrm(ks[6], (L, 2 * D), 0.01),
        "w_branch_fox": nrm(ks[7], (L, FOX_WIDTH, D), FOX_WIDTH ** -0.5),
        "w_branch_sb": nrm(ks[8], (L, SB_WIDTH, D), SB_WIDTH ** -0.5),
        "w_out": nrm(ks[9], (L, D, D), D ** -0.5 * DEEPNORM_BETA),
        "ln1_g": 1.0 + nrm(ks[10], (L, D), 0.02),
        "ln1_b": nrm(ks[11], (L, D), 0.01),
        "w_router": nrm(ks[12], (L, D, N_EXPERTS), D ** -0.5),
        "router_bias": nrm(ks[13], (L, N_EXPERTS), 0.01),
        "w_exp_gate": nrm(ks[14], (L, N_EXPERTS, D, EXPERT_DIM), D ** -0.5),
        "w_exp_up": nrm(ks[15], (L, N_EXPERTS, D, EXPERT_DIM), D ** -0.5 * DEEPNORM_BETA),
        "w_exp_down": nrm(ks[16], (L, N_EXPERTS, EXPERT_DIM, D), EXPERT_DIM ** -0.5 * DEEPNORM_BETA),
        "w_sh_gate": nrm(ks[17], (L, D, SHARED_DIM), D ** -0.5),
        "w_sh_up": nrm(ks[18], (L, D, SHARED_DIM), D ** -0.5 * DEEPNORM_BETA),
        "w_sh_down": nrm(ks[19], (L, SHARED_DIM, D), SHARED_DIM ** -0.5 * DEEPNORM_BETA),
        "ln2_g": 1.0 + nrm(ks[20], (L, D), 0.02),
        "ln2_b": nrm(ks[21], (L, D), 0.01),
    }


def reference(x, c, w_ada, b_ada, w_in, b_forget, b_gate, w_branch_fox, w_branch_sb, w_out,
              ln1_g, ln1_b, w_router, router_bias, w_exp_gate, w_exp_up, w_exp_down,
              w_sh_gate, w_sh_up, w_sh_down, ln2_g, ln2_b):
    for l in range(DEPTH):
        mod = jax.nn.silu(c) @ w_ada[l] + b_ada[l]
        sh1, sc1, g1, sh2, sc2, g2 = jnp.split(mod[:, None, :], 6, axis=-1)
        h = x * (1.0 + sc1) + sh1
        mix = _token_mixer(h, w_in[l], b_forget[l], b_gate[l], w_branch_fox[l], w_branch_sb[l], w_out[l])
        x = _layer_norm(DEEPNORM_ALPHA * x + g1 * mix, ln1_g[l], ln1_b[l])
        h = x * (1.0 + sc2) + sh2
        ffn = _moe(h, w_router[l], router_bias[l], w_exp_gate[l], w_exp_up[l], w_exp_down[l],
                   w_sh_gate[l], w_sh_up[l], w_sh_down[l])
        x = _layer_norm(DEEPNORM_ALPHA * x + g2 * ffn, ln2_g[l], ln2_b[l])
    return x
```

```python
import functools

import jax
import jax.numpy as jnp
from jax import lax
from jax.experimental import pallas as pl
from jax.experimental.pallas import tpu as pltpu

F32 = jnp.float32
BF16 = jnp.bfloat16
I32 = jnp.int32

HEAD_DIM = 128
LANES = 128
N_GROUPS = 8
TOPK_GROUPS = 4
TOP_K = 8
ROUTED_SCALE = 2.5
LN_EPS = 1e-5
VMEM_LIMIT = 56 * 1024 * 1024
NEG_BIG = -1e30


def _cparams(sem):
    return pltpu.CompilerParams(dimension_semantics=sem, vmem_limit_bytes=VMEM_LIMIT)


def _pick(n, pref):
    t = min(n, pref)
    while n % t:
        t //= 2
    return t


def _dot(a, b):
    return jnp.dot(a, b, preferred_element_type=F32)


def _dot_nt(a, b, precision=None):
    return lax.dot_general(a, b, (((1,), (1,)), ((), ())), preferred_element_type=F32,
                           precision=precision)


def _softplus_parts(z):
    lg = jnp.log(1.0 + jnp.exp(-jnp.abs(z)))
    return jnp.maximum(z, 0.0) + lg, jnp.minimum(z, 0.0) - lg


def _ada_kernel(c_ref, w_ref, b_ref, o_ref):
    c = c_ref[...]
    s = c * (1.0 / (1.0 + jnp.exp(-c)))
    o_ref[...] = jnp.dot(s, w_ref[...], preferred_element_type=F32,
                         precision=lax.Precision.HIGHEST) + b_ref[...]


def _ada(c, w_ada, b_ada):
    bsz, d = c.shape
    n = w_ada.shape[1]
    rows = 8
    c_pad = jnp.zeros((rows, d), F32).at[:bsz].set(c)
    tn = _pick(n, 512)
    out = pl.pallas_call(
        _ada_kernel,
        grid=(n // tn,),
        in_specs=[pl.BlockSpec((rows, d), lambda j: (0, 0)),
                  pl.BlockSpec((d, tn), lambda j: (0, j)),
                  pl.BlockSpec((1, tn), lambda j: (0, j))],
        out_specs=pl.BlockSpec((rows, tn), lambda j: (0, j)),
        out_shape=jax.ShapeDtypeStruct((rows, n), F32),
        compiler_params=_cparams(("parallel",)),
        name="ada_mod",
    )(c_pad, w_ada, b_ada.reshape(1, n))
    return out[:bsz]


def _inproj_kernel(x_ref, sc_ref, sh_ref, w_ref, b_ref, wf_ref, o_ref, f_ref, h_ref, *, n_plain):
    j = pl.program_id(1)

    @pl.when(j == 0)
    def _():
        h = (x_ref[...] * (1.0 + sc_ref[...]) + sh_ref[...]).astype(BF16)
        h_ref[...] = h
        f_ref[...] = _dot(h, wf_ref[...])

    acc = _dot(h_ref[...], w_ref[...])

    @pl.when(j < n_plain)
    def _():
        o_ref[...] = acc.astype(o_ref.dtype)

    @pl.when(j >= n_plain)
    def _():
        z = acc + b_ref[...]
        o_ref[...] = (1.0 / (1.0 + jnp.exp(-z))).astype(o_ref.dtype)


def _inproj(x2, sc, sh, w_packed, bias_packed, wf, seq, n_plain_cols):
    t, d = x2.shape
    n = w_packed.shape[1]
    tm = _pick(seq, 512)
    tn = _pick(n_plain_cols, 1024)
    while (n - n_plain_cols) % tn:
        tn //= 2
    per_b = seq // tm
    return pl.pallas_call(
        functools.partial(_inproj_kernel, n_plain=n_plain_cols // tn),
        grid=(t // tm, n // tn),
        in_specs=[pl.BlockSpec((tm, d), lambda i, j: (i, 0)),
                  pl.BlockSpec((None, 1, d), lambda i, j: (i // per_b, 0, 0)),
                  pl.BlockSpec((None, 1, d), lambda i, j: (i // per_b, 0, 0)),
                  pl.BlockSpec((d, tn), lambda i, j: (0, j)),
                  pl.BlockSpec((1, tn), lambda i, j: (0, j)),
                  pl.BlockSpec((d, LANES), lambda i, j: (0, 0))],
        out_specs=[pl.BlockSpec((tm, tn), lambda i, j: (i, j)),
                   pl.BlockSpec((tm, LANES), lambda i, j: (i, 0))],
        out_shape=[jax.ShapeDtypeStruct((t, n), BF16),
                   jax.ShapeDtypeStruct((t, LANES), F32)],
        scratch_shapes=[pltpu.VMEM((tm, d), BF16)],
        compiler_params=_cparams(("parallel", "arbitrary")),
        name="in_proj",
    )(x2, sc, sh, w_packed, bias_packed, wf)


def _cum_kernel(f_ref, b_ref, o_ref, carry_ref, *, per_b):
    i = pl.program_id(0)

    @pl.when(i % per_b == 0)
    def _():
        carry_ref[...] = jnp.zeros_like(carry_ref)

    z = f_ref[...] + b_ref[...]
    lf = jnp.minimum(z, 0.0) - jnp.log(1.0 + jnp.exp(-jnp.abs(z)))
    tb = lf.shape[0]
    row = lax.broadcasted_iota(I32, (tb, tb), 0)
    col = lax.broadcasted_iota(I32, (tb, tb), 1)
    tri = jnp.where(col <= row, 1.0, 0.0).astype(F32)
    cum = jnp.dot(tri, lf, preferred_element_type=F32,
                  precision=lax.Precision.HIGHEST) + carry_ref[...]
    o_ref[...] = cum
    carry_ref[...] = cum[tb - 1:tb, :]


def _forget_cumsum(fa, bf_pad, seq):
    t = fa.shape[0]
    tb = _pick(seq, 512)
    return pl.pallas_call(
        functools.partial(_cum_kernel, per_b=seq // tb),
        grid=(t // tb,),
        in_specs=[pl.BlockSpec((tb, LANES), lambda i: (i, 0)),
                  pl.BlockSpec((1, LANES), lambda i: (0, 0))],
        out_specs=pl.BlockSpec((tb, LANES), lambda i: (i, 0)),
        out_shape=jax.ShapeDtypeStruct((t, LANES), F32),
        scratch_shapes=[pltpu.VMEM((1, LANES), F32)],
        compiler_params=_cparams(("arbitrary",)),
        name="forget_cumsum",
    )(fa, bf_pad)


def _fox_kernel(q_ref, k_ref, v_ref, ck_ref, o_ref, *, tq, scale):
    i = pl.program_id(2)
    q = q_ref[...]
    q0 = pl.multiple_of(i * tq, tq)
    c0 = ck_ref[:, pl.ds(q0, tq)][:, 0:1]

    def scores(k0):
        k = k_ref[pl.ds(k0, tq), :]
        s = _dot_nt(q, k) * scale
        return s + (c0 - ck_ref[:, pl.ds(k0, tq)])

    def update(s, k0, carry):
        m, l, acc = carry
        m_new = jnp.maximum(m, jnp.max(s, axis=1, keepdims=True))
        p = jnp.exp(s - m_new)
        alpha = jnp.exp(m - m_new)
        l = alpha * l + jnp.sum(p, axis=1, keepdims=True)
        acc = alpha * acc + _dot(p.astype(BF16), v_ref[pl.ds(k0, tq), :])
        return m_new, l, acc

    def body(j, carry):
        k0 = pl.multiple_of(j * tq, tq)
        return update(scores(k0), k0, carry)

    init = (jnp.full((tq, 1), NEG_BIG, F32), jnp.zeros((tq, 1), F32),
            jnp.zeros((tq, HEAD_DIM), F32))
    carry = lax.fori_loop(0, i, body, init)
    row = lax.broadcasted_iota(I32, (tq, tq), 0)
    col = lax.broadcasted_iota(I32, (tq, tq), 1)
    s = jnp.where(col <= row, scores(q0), NEG_BIG)
    _, l, acc = update(s, q0, carry)
    o_ref[...] = (acc / l).astype(o_ref.dtype)


def _fox_attention(proj, cum_rows, bsz, seq, heads, q_blk, k_blk, v_blk):
    tq = _pick(seq, 512)
    nq = seq // tq
    return pl.pallas_call(
        functools.partial(_fox_kernel, tq=tq, scale=HEAD_DIM ** -0.5),
        grid=(bsz, heads, nq),
        in_specs=[pl.BlockSpec((tq, HEAD_DIM), lambda b, h, i: (b * nq + i, q_blk + h)),
                  pl.BlockSpec((seq, HEAD_DIM), lambda b, h, i: (b, k_blk + h)),
                  pl.BlockSpec((seq, HEAD_DIM), lambda b, h, i: (b, v_blk + h)),
                  pl.BlockSpec((None, 1, seq), lambda b, h, i: (b * heads + h, 0, 0))],
        out_specs=pl.BlockSpec((tq, HEAD_DIM), lambda b, h, i: (b * nq + i, h)),
        out_shape=jax.ShapeDtypeStruct((bsz * seq, heads * HEAD_DIM), BF16),
        compiler_params=_cparams(("parallel", "parallel", "arbitrary")),
        name="fox_attention",
    )(proj, proj, proj, cum_rows)


def _sb_kernel(q_ref, k_ref, v_ref, o_ref, *, tq, scale):
    i = pl.program_id(2)
    q = q_ref[...]
    q0 = pl.multiple_of(i * tq, tq)
    row = lax.broadcasted_iota(I32, (tq, tq), 0)
    col = lax.broadcasted_iota(I32, (tq, tq), 1)
    later = jnp.where(row > col, 1.0, 0.0).astype(BF16)
    valid = col < row

    def block(k0):
        z = _dot_nt(q, k_ref[pl.ds(k0, tq), :]) * scale
        return _softplus_parts(z)

    sp, ls = block(q0)
    sp = jnp.where(valid, sp, 0.0)
    tail = _dot(sp.astype(BF16), later)
    w = jnp.where(valid, jnp.exp(ls - tail), 0.0)
    acc = _dot(w.astype(BF16), v_ref[pl.ds(q0, tq), :])
    run = tail[:, 0:1] + sp[:, 0:1]

    def body(n, carry):
        run, acc = carry
        k0 = pl.multiple_of((i - 1 - n) * tq, tq)
        sp, ls = block(k0)
        tail = _dot(sp.astype(BF16), later)
        w = jnp.exp(ls - tail - run)
        acc = acc + _dot(w.astype(BF16), v_ref[pl.ds(k0, tq), :])
        return run + tail[:, 0:1] + sp[:, 0:1], acc

    _, acc = lax.fori_loop(0, i, body, (run, acc))
    o_ref[...] = acc.astype(o_ref.dtype)


def _sb_attention(proj, bsz, seq, heads, q_blk, k_blk, v_blk):
    tq = _pick(seq, 256)
    nq = seq // tq
    return pl.pallas_call(
        functools.partial(_sb_kernel, tq=tq, scale=HEAD_DIM ** -0.5),
        grid=(bsz, heads, nq),
        in_specs=[pl.BlockSpec((tq, HEAD_DIM), lambda b, h, i: (b * nq + i, q_blk + h)),
                  pl.BlockSpec((seq, HEAD_DIM), lambda b, h, i: (b, k_blk + h)),
                  pl.BlockSpec((seq, HEAD_DIM), lambda b, h, i: (b, v_blk + h))],
        out_specs=pl.BlockSpec((tq, HEAD_DIM), lambda b, h, i: (b * nq + i, h)),
        out_shape=jax.ShapeDtypeStruct((bsz * seq, heads * HEAD_DIM), BF16),
        compiler_params=_cparams(("parallel", "parallel", "arbitrary")),
        name="sb_attention",
    )(proj, proj, proj)


def _merge_kernel(of_ref, os_ref, wa_ref, wb_ref, gf_ref, gs_ref, o_ref):
    yf = _dot(of_ref[...], wa_ref[...])
    ys = _dot(os_ref[...], wb_ref[...])
    o_ref[...] = (gf_ref[...].astype(F32) * yf + gs_ref[...].astype(F32) * ys).astype(o_ref.dtype)


def _merge(o_fox, o_sb, wa, wb, proj, g_col0, d):
    t = o_fox.shape[0]
    tm = _pick(t, 512)
    tn = _pick(d, 1024)
    gf0 = g_col0 // tn
    gs0 = (g_col0 + d) // tn
    return pl.pallas_call(
        _merge_kernel,
        grid=(t // tm, d // tn),
        in_specs=[pl.BlockSpec((tm, o_fox.shape[1]), lambda i, j: (i, 0)),
                  pl.BlockSpec((tm, o_sb.shape[1]), lambda i, j: (i, 0)),
                  pl.BlockSpec((wa.shape[0], tn), lambda i, j: (0, j)),
                  pl.BlockSpec((wb.shape[0], tn), lambda i, j: (0, j)),
                  pl.BlockSpec((tm, tn), lambda i, j: (i, gf0 + j)),
                  pl.BlockSpec((tm, tn), lambda i, j: (i, gs0 + j))],
        out_specs=pl.BlockSpec((tm, tn), lambda i, j: (i, j)),
        out_shape=jax.ShapeDtypeStruct((t, d), BF16),
        compiler_params=_cparams(("parallel", "arbitrary")),
        name="branch_merge",
    )(o_fox, o_sb, wa, wb, proj, proj)


def _layer_norm(v, g, b):
    mu = jnp.mean(v, axis=-1, keepdims=True)
    c = v - mu
    var = jnp.mean(c * c, axis=-1, keepdims=True)
    return c * lax.rsqrt(var + LN_EPS) * g + b


def _outln_kernel(m_ref, w_ref, x_ref, g1_ref, lg_ref, lb_ref, sc_ref, sh_ref,
                  x1_ref, h2_ref, y_ref, *, tn, alpha):
    j = pl.program_id(1)
    y_ref[:, pl.ds(pl.multiple_of(j * tn, tn), tn)] = _dot(m_ref[...], w_ref[...])

    @pl.when(j == pl.num_programs(1) - 1)
    def _():
        v = alpha * x_ref[...] + g1_ref[...] * y_ref[...]
        x1 = _layer_norm(v, lg_ref[...], lb_ref[...])
        x1_ref[...] = x1
        h2_ref[...] = x1 * (1.0 + sc_ref[...]) + sh_ref[...]


def _out_ln(merged, w_out, x2, g1, ln_g, ln_b, sc2, sh2, seq, alpha):
    t, d = x2.shape
    tm = _pick(seq, 256)
    tn = _pick(d, 512)
    per_b = seq // tm
    row = lambda i, j: (i, 0)
    per_batch = lambda i, j: (i // per_b, 0, 0)
    const = lambda i, j: (0, 0)
    return pl.pallas_call(
        functools.partial(_outln_kernel, tn=tn, alpha=alpha),
        grid=(t // tm, d // tn),
        in_specs=[pl.BlockSpec((tm, d), row),
                  pl.BlockSpec((d, tn), lambda i, j: (0, j)),
                  pl.BlockSpec((tm, d), row),
                  pl.BlockSpec((None, 1, d), per_batch),
                  pl.BlockSpec((1, d), const),
                  pl.BlockSpec((1, d), const),
                  pl.BlockSpec((None, 1, d), per_batch),
                  pl.BlockSpec((None, 1, d), per_batch)],
        out_specs=[pl.BlockSpec((tm, d), row), pl.BlockSpec((tm, d), row)],
        out_shape=[jax.ShapeDtypeStruct((t, d), F32), jax.ShapeDtypeStruct((t, d), F32)],
        scratch_shapes=[pltpu.VMEM((tm, d), F32)],
        compiler_params=_cparams(("parallel", "arbitrary")),
        name="out_proj_ln1",
    )(merged, w_out, x2, g1, ln_g, ln_b, sc2, sh2)


def _route_kernel(h_ref, wr_ref, rb_ref, idx_ref, rank_ref, w_ref, cnt_ref, carry_ref, *, n_exp):
    i = pl.program_id(0)

    @pl.when(i == 0)
    def _():
        carry_ref[...] = jnp.zeros_like(carry_ref)

    tb = h_ref.shape[0]
    gsz = n_exp // N_GROUPS
    logits = _dot_nt(wr_ref[...], h_ref[...], precision=lax.Precision.HIGHEST)
    scores = 1.0 / (1.0 + jnp.exp(-logits))
    choice = scores + rb_ref[...][:, 0:1]
    neg = -jnp.inf

    g = choice.reshape(N_GROUPS, gsz, tb)
    io = lax.broadcasted_iota(I32, g.shape, 1)
    m1 = jnp.max(g, axis=1, keepdims=True)
    first = jnp.min(jnp.where(g == m1, io, gsz), axis=1, keepdims=True)
    m2 = jnp.max(jnp.where(io == first, neg, g), axis=1, keepdims=True)
    gs = (m1 + m2).reshape(N_GROUPS, tb)

    gio = lax.broadcasted_iota(I32, gs.shape, 0)
    gsel = jnp.zeros(gs.shape, F32)
    for _ in range(TOPK_GROUPS):
        m = jnp.max(gs, axis=0, keepdims=True)
        f = jnp.min(jnp.where(gs == m, gio, N_GROUPS), axis=0, keepdims=True)
        hit = gio == f
        gsel = jnp.where(hit, 1.0, gsel)
        gs = jnp.where(hit, neg, gs)
    emask = jnp.broadcast_to(gsel.reshape(N_GROUPS, 1, tb), (N_GROUPS, gsz, tb)).reshape(n_exp, tb)

    v = jnp.where(emask > 0.5, choice, neg)
    eio = lax.broadcasted_iota(I32, v.shape, 0)
    sel = jnp.zeros(v.shape, F32)
    idx_rows, w_rows = [], []
    for _ in range(TOP_K):
        m = jnp.max(v, axis=0, keepdims=True)
        f = jnp.min(jnp.where(v == m, eio, n_exp), axis=0, keepdims=True)
        hit = eio == f
        idx_rows.append(f)
        w_rows.append(jnp.sum(jnp.where(hit, scores, 0.0), axis=0, keepdims=True))
        sel = jnp.where(hit, 1.0, sel)
        v = jnp.where(hit, neg, v)
    wsum = w_rows[0]
    for r in range(1, TOP_K):
        wsum = wsum + w_rows[r]
    inv = 1.0 / (wsum + 1e-20)

    r_io = lax.broadcasted_iota(I32, (tb, tb), 0)
    c_io = lax.broadcasted_iota(I32, (tb, tb), 1)
    before = jnp.where(r_io < c_io, 1.0, 0.0).astype(BF16)
    excl = _dot(sel.astype(BF16), before) + carry_ref[...][:, 0:1]
    for r in range(TOP_K):
        hit = eio == idx_rows[r]
        rank = jnp.sum(jnp.where(hit, excl, 0.0), axis=0, keepdims=True)
        idx_ref[r:r + 1, :] = idx_rows[r]
        rank_ref[r:r + 1, :] = rank.astype(I32)
        w_ref[r:r + 1, :] = w_rows[r] * inv * ROUTED_SCALE
    total = carry_ref[...] + jnp.sum(sel, axis=1, keepdims=True)
    carry_ref[...] = total
    cnt_ref[...] = total.astype(I32)


def _route(h2, wr_t, rb):
    t, d = h2.shape
    n_exp = wr_t.shape[0]
    tb = _pick(t, 512)
    blk = pl.BlockSpec((TOP_K, tb), lambda i: (0, i))
    return pl.pallas_call(
        functools.partial(_route_kernel, n_exp=n_exp),
        grid=(t // tb,),
        in_specs=[pl.BlockSpec((tb, d), lambda i: (i, 0)),
                  pl.BlockSpec((n_exp, d), lambda i: (0, 0)),
                  pl.BlockSpec((n_exp, LANES), lambda i: (0, 0))],
        out_specs=[blk, blk, blk, pl.BlockSpec((n_exp, LANES), lambda i: (0, 0))],
        out_shape=[jax.ShapeDtypeStruct((TOP_K, t), I32),
                   jax.ShapeDtypeStruct((TOP_K, t), I32),
                   jax.ShapeDtypeStruct((TOP_K, t), F32),
                   jax.ShapeDtypeStruct((n_exp, LANES), I32)],
        scratch_shapes=[pltpu.VMEM((n_exp, LANES), F32)],
        compiler_params=_cparams(("arbitrary",)),
        name="router_topk",
    )(h2, wr_t, rb)


def _plan_kernel(idx_ref, rank_ref, cnt_ref, dest_ref, tok_ref, be_ref, meta_ref, start_ref,
                 *, n_exp, tm, chunk, n_tok, n_rows, n_blocks):
    c = pl.program_id(0)

    @pl.when(c == 0)
    def _():
        def fill_tok(r, _):
            tok_ref[r] = 0
            return 0
        lax.fori_loop(0, n_rows, fill_tok, 0)

        def per_expert(e, blk0):
            nb = (cnt_ref[e] + (tm - 1)) // tm
            start_ref[e] = blk0 * tm

            def fill_be(n, _):
                be_ref[blk0 + n] = e
                return 0
            lax.fori_loop(0, nb, fill_be, 0)
            return blk0 + nb
        used = lax.fori_loop(0, n_exp, per_expert, 0)

        def fill_rest(b, _):
            be_ref[b] = n_exp - 1
            return 0
        lax.fori_loop(used, n_blocks, fill_rest, 0)
        meta_ref[0] = used

    tok0 = (c * chunk) % n_tok

    def place(n, _):
        d = start_ref[idx_ref[n]] + rank_ref[n]
        dest_ref[n] = d
        tok_ref[d] = tok0 + n
        return 0
    lax.fori_loop(0, chunk, place, 0)


def _plan(idx_flat, rank_flat, counts, n_tok, n_exp, tm):
    n_assign = idx_flat.shape[0]
    n_rows = n_assign + n_exp * tm
    n_blocks = n_rows // tm
    chunk = _pick(n_tok, 8192)
    smem = pltpu.SMEM
    return pl.pallas_call(
        functools.partial(_plan_kernel, n_exp=n_exp, tm=tm, chunk=chunk, n_tok=n_tok,
                          n_rows=n_rows, n_blocks=n_blocks),
        grid=(n_assign // chunk,),
        in_specs=[pl.BlockSpec((chunk,), lambda c: (c,), memory_space=smem),
                  pl.BlockSpec((chunk,), lambda c: (c,), memory_space=smem),
                  pl.BlockSpec(memory_space=smem)],
        out_specs=[pl.BlockSpec((chunk,), lambda c: (c,), memory_space=smem),
                   pl.BlockSpec(memory_space=smem),
                   pl.BlockSpec(memory_space=smem),
                   pl.BlockSpec(memory_space=smem)],
        out_shape=[jax.ShapeDtypeStruct((n_assign,), I32),
                   jax.ShapeDtypeStruct((n_rows,), I32),
                   jax.ShapeDtypeStruct((n_blocks,), I32),
                   jax.ShapeDtypeStruct((1,), I32)],
        scratch_shapes=[pltpu.SMEM((n_exp,), I32)],
        compiler_params=_cparams(("arbitrary",)),
        name="dispatch_plan",
    )(idx_flat, rank_flat, counts)


def _moe_kernel(tok_ref, be_ref, meta_ref, h_hbm, wg_ref, wu_ref, wd_ref, y_ref, xbuf, sem, *, tm):
    b = pl.program_id(0)
    used = meta_ref[0]

    def row_copy(blk, slot, r):
        tok = tok_ref[blk * tm + r]
        return pltpu.make_async_copy(h_hbm.at[pl.ds(tok, 1)], xbuf.at[slot, pl.ds(r, 1)],
                                     sem.at[slot])

    def gather(blk, slot):
        def issue(r, _):
            row_copy(blk, slot, r).start()
            return 0
        lax.fori_loop(0, tm, issue, 0)

    @pl.when(b == 0)
    def _():
        gather(0, 0)

    @pl.when(b + 1 < used)
    def _():
        gather(b + 1, (b + 1) % 2)

    @pl.when(b < used)
    def _():
        slot = b % 2

        def drain(r, _):
            row_copy(b, slot, r).wait()
            return 0
        lax.fori_loop(0, tm, drain, 0)
        x = xbuf[slot].astype(BF16)
        g = _dot(x, wg_ref[...])
        u = _dot(x, wu_ref[...])
        a = (g * (1.0 / (1.0 + jnp.exp(-g))) * u).astype(BF16)
        y_ref[...] = _dot(a, wd_ref[...])

    @pl.when(b >= used)
    def _():
        y_ref[...] = jnp.zeros_like(y_ref)


def _moe(tok_buf, block_e, meta, h2, wg, wu, wd, tm):
    n_rows = tok_buf.shape[0]
    d = h2.shape[1]
    f = wg.shape[2]
    grid_spec = pltpu.PrefetchScalarGridSpec(
        num_scalar_prefetch=3,
        grid=(n_rows // tm,),
        in_specs=[pl.BlockSpec(memory_space=pl.ANY),
                  pl.BlockSpec((None, d, f), lambda b, tok, be, meta: (be[b], 0, 0)),
                  pl.BlockSpec((None, d, f), lambda b, tok, be, meta: (be[b], 0, 0)),
                  pl.BlockSpec((None, f, d), lambda b, tok, be, meta: (be[b], 0, 0))],
        out_specs=pl.BlockSpec((tm, d), lambda b, tok, be, meta: (b, 0)),
        scratch_shapes=[pltpu.VMEM((2, tm, d), F32), pltpu.SemaphoreType.DMA((2,))],
    )
    return pl.pallas_call(
        functools.partial(_moe_kernel, tm=tm),
        grid_spec=grid_spec,
        out_shape=jax.ShapeDtypeStruct((n_rows, d), F32),
        compiler_params=_cparams(("arbitrary",)),
        name="routed_experts",
    )(tok_buf, block_e, meta, h2, wg, wu, wd)


def _shared_kernel(h_ref, wg_ref, wu_ref, wd_ref, o_ref):
    x = h_ref[...].astype(BF16)
    g = _dot(x, wg_ref[...])
    u = _dot(x, wu_ref[...])
    a = (g * (1.0 / (1.0 + jnp.exp(-g))) * u).astype(BF16)
    o_ref[...] = _dot(a, wd_ref[...]).astype(o_ref.dtype)


def _shared(h2, wg, wu, wd):
    t, d = h2.shape
    f = wg.shape[1]
    tm = _pick(t, 256)
    return pl.pallas_call(
        _shared_kernel,
        grid=(t // tm,),
        in_specs=[pl.BlockSpec((tm, d), lambda i: (i, 0)),
                  pl.BlockSpec((d, f), lambda i: (0, 0)),
                  pl.BlockSpec((d, f), lambda i: (0, 0)),
                  pl.BlockSpec((f, d), lambda i: (0, 0))],
        out_specs=pl.BlockSpec((tm, d), lambda i: (i, 0)),
        out_shape=jax.ShapeDtypeStruct((t, d), BF16),
        compiler_params=_cparams(("parallel",)),
        name="shared_expert",
    )(h2, wg, wu, wd)


def _final_kernel(dest_ref, y_hbm, w_ref, x1_ref, s_ref, g2_ref, lg_ref, lb_ref, o_ref, ybuf, sem,
                  *, tb, n_tok, alpha):
    i = pl.program_id(0)
    nsteps = pl.num_programs(0)

    def row_copy(blk, slot, r, n):
        row = dest_ref[r * n_tok + blk * tb + n]
        return pltpu.make_async_copy(y_hbm.at[pl.ds(row, 1)], ybuf.at[slot, r, pl.ds(n, 1)],
                                     sem.at[slot])

    def gather(blk, slot):
        for r in range(TOP_K):
            def issue(n, _):
                row_copy(blk, slot, r, n).start()
                return 0
            lax.fori_loop(0, tb, issue, 0)

    @pl.when(i == 0)
    def _():
        gather(0, 0)

    @pl.when(i + 1 < nsteps)
    def _():
        gather(i + 1, (i + 1) % 2)

    slot = i % 2
    for r in range(TOP_K):
        def drain(n, _):
            row_copy(i, slot, r, n).wait()
            return 0
        lax.fori_loop(0, tb, drain, 0)

    w = w_ref[...]
    routed = w[:, 0:1] * ybuf[slot, 0]
    for r in range(1, TOP_K):
        routed = routed + w[:, r:r + 1] * ybuf[slot, r]
    ffn = routed + s_ref[...].astype(F32)
    v = alpha * x1_ref[...] + g2_ref[...] * ffn
    o_ref[...] = _layer_norm(v, lg_ref[...], lb_ref[...])


def _final(dest, ys, w_tok, x1, shared, g2, ln_g, ln_b, seq, alpha):
    t, d = x1.shape
    tb = _pick(seq, 64)
    per_b = seq // tb
    grid_spec = pltpu.PrefetchScalarGridSpec(
        num_scalar_prefetch=1,
        grid=(t // tb,),
        in_specs=[pl.BlockSpec(memory_space=pl.ANY),
                  pl.BlockSpec((tb, TOP_K), lambda i, dest: (i, 0)),
                  pl.BlockSpec((tb, d), lambda i, dest: (i, 0)),
                  pl.BlockSpec((tb, d), lambda i, dest: (i, 0)),
                  pl.BlockSpec((None, 1, d), lambda i, dest: (i // per_b, 0, 0)),
                  pl.BlockSpec((1, d), lambda i, dest: (0, 0)),
                  pl.BlockSpec((1, d), lambda i, dest: (0, 0))],
        out_specs=pl.BlockSpec((tb, d), lambda i, dest: (i, 0)),
        scratch_shapes=[pltpu.VMEM((2, TOP_K, tb, d), F32), pltpu.SemaphoreType.DMA((2,))],
    )
    return pl.pallas_call(
        functools.partial(_final_kernel, tb=tb, n_tok=t, alpha=alpha),
        grid_spec=grid_spec,
        out_shape=jax.ShapeDtypeStruct((t, d), F32),
        compiler_params=_cparams(("arbitrary",)),
        name="combine_ln2",
    )(dest, ys, w_tok, x1, shared, g2, ln_g, ln_b)


def _layer(x2, c, bsz, seq, w_ada, b_ada, w_in, b_forget, b_gate, w_branch_fox, w_branch_sb, w_out,
           ln1_g, ln1_b, w_router, router_bias, w_exp_gate, w_exp_up, w_exp_down,
           w_sh_gate, w_sh_up, w_sh_down, ln2_g, ln2_b, alpha):
    t, d = x2.shape
    hf = b_forget.shape[0]
    fw = w_branch_fox.shape[0]
    sw = w_branch_sb.shape[0]
    hs = sw // HEAD_DIM
    n_exp = w_router.shape[1]

    mod = _ada(c, w_ada, b_ada)
    sh1, sc1, g1, sh2, sc2, g2 = [m.reshape(bsz, 1, d) for m in jnp.split(mod, 6, axis=-1)]

    f0 = 3 * fw
    qkv_cols = 3 * fw + 3 * sw
    w_packed = jnp.concatenate([w_in[:, :f0], w_in[:, f0 + hf:]], axis=1).astype(BF16)
    bias_packed = jnp.concatenate([jnp.zeros((qkv_cols,), F32), b_gate]).reshape(1, -1)
    wf = jnp.zeros((d, LANES), F32).at[:, :hf].set(w_in[:, f0:f0 + hf]).astype(BF16)
    bf_pad = jnp.zeros((1, LANES), F32).at[0, :hf].set(b_forget)

    proj, fa = _inproj(x2, sc1, sh1, w_packed, bias_packed, wf, seq, qkv_cols)
    cum = _forget_cumsum(fa, bf_pad, seq)
    cum_rows = cum[:, :hf].reshape(bsz, seq, hf).transpose(0, 2, 1).reshape(bsz * hf, 1, seq)

    o_fox = _fox_attention(proj, cum_rows, bsz, seq, hf, 0, hf, 2 * hf)
    o_sb = _sb_attention(proj, bsz, seq, hs, 3 * hf, 3 * hf + hs, 3 * hf + 2 * hs)
    merged = _merge(o_fox, o_sb, w_branch_fox.astype(BF16), w_branch_sb.astype(BF16), proj,
                    qkv_cols, d)
    x1, h2 = _out_ln(merged, w_out.astype(BF16), x2, g1, ln1_g.reshape(1, d), ln1_b.reshape(1, d),
                     sc2, sh2, seq, alpha)

    rb = jnp.broadcast_to(router_bias.reshape(n_exp, 1), (n_exp, LANES))
    idx, rank, w_top, counts = _route(h2, w_router.T, rb)
    tm = 256
    dest, tok_buf, block_e, meta = _plan(idx.reshape(-1), rank.reshape(-1), counts[:, 0], t, n_exp, tm)
    ys = _moe(tok_buf, block_e, meta, h2, w_exp_gate.astype(BF16), w_exp_up.astype(BF16),
              w_exp_down.astype(BF16), tm)
    shared = _shared(h2, w_sh_gate.astype(BF16), w_sh_up.astype(BF16), w_sh_down.astype(BF16))
    return _final(dest, ys, w_top.T, x1, shared, g2, ln2_g.reshape(1, d), ln2_b.reshape(1, d),
                  seq, alpha)


def kernel(x, c, w_ada, b_ada, w_in, b_forget, b_gate, w_branch_fox, w_branch_sb, w_out, ln1_g, ln1_b, w_router, router_bias, w_exp_gate, w_exp_up, w_exp_down, w_sh_gate, w_sh_up, w_sh_down, ln2_g, ln2_b):
    bsz, seq, d = x.shape
    depth = w_ada.shape[0]
    alpha = (2 * depth) ** 0.25
    x2 = x.reshape(bsz * seq, d)
    for l in range(depth):
        x2 = _layer(x2, c, bsz, seq, w_ada[l], b_ada[l], w_in[l], b_forget[l], b_gate[l],
                    w_branch_fox[l], w_branch_sb[l], w_out[l], ln1_g[l], ln1_b[l], w_router[l],
                    router_bias[l], w_exp_gate[l], w_exp_up[l], w_exp_down[l], w_sh_gate[l],
                    w_sh_up[l], w_sh_down[l], ln2_g[l], ln2_b[l], alpha)
    return x2.reshape(bsz, seq, d)
```

```python
import functools

import jax
import jax.numpy as jnp
from jax import lax
from jax.experimental import pallas as pl
from jax.experimental.pallas import tpu as pltpu

F32 = jnp.float32
BF16 = jnp.bfloat16
I32 = jnp.int32

HEAD_DIM = 128
LANES = 128
N_GROUPS = 8
TOPK_GROUPS = 4
TOP_K = 8
ROUTED_SCALE = 2.5
LN_EPS = 1e-5
VMEM_LIMIT = 56 * 1024 * 1024
NEG_BIG = -1e30
LOG2E = 1.4426950408889634
FOX_TQ, FOX_HP = 512, 2
SB_TQ, SB_TK, SB_HP = 512, 256, 4
MOE_TM = 256
COMBINE_TB = 64
DMA_UNROLL = 8


def _cparams(sem, **kw):
    return pltpu.CompilerParams(dimension_semantics=sem, vmem_limit_bytes=VMEM_LIMIT, **kw)


def _pick(n, pref):
    t = min(n, pref)
    while n % t:
        t //= 2
    return t


def _dot(a, b):
    return jnp.dot(a, b, preferred_element_type=F32)


def _dot_nt(a, b, precision=None):
    return lax.dot_general(a, b, (((1,), (1,)), ((), ())), preferred_element_type=F32,
                           precision=precision)


def _ada_kernel(c_ref, w_ref, b_ref, o_ref):
    c = c_ref[...]
    s = c * (1.0 / (1.0 + jnp.exp(-c)))
    o_ref[...] = jnp.dot(s, w_ref[...], preferred_element_type=F32,
                         precision=lax.Precision.HIGHEST) + b_ref[...]


def _ada(c, w_ada, b_ada):
    bsz, d = c.shape
    n = w_ada.shape[1]
    rows = 8
    c_pad = jnp.zeros((rows, d), F32).at[:bsz].set(c)
    tn = _pick(n, 512)
    out = pl.pallas_call(
        _ada_kernel,
        grid=(n // tn,),
        in_specs=[pl.BlockSpec((rows, d), lambda j: (0, 0)),
                  pl.BlockSpec((d, tn), lambda j: (0, j)),
                  pl.BlockSpec((1, tn), lambda j: (0, j))],
        out_specs=pl.BlockSpec((rows, tn), lambda j: (0, j)),
        out_shape=jax.ShapeDtypeStruct((rows, n), F32),
        compiler_params=_cparams(("parallel",)),
        name="ada_mod",
    )(c_pad, w_ada, b_ada.reshape(1, n))
    return out[:bsz]


def _inproj_kernel(x_ref, sc_ref, sh_ref, w_ref, b_ref, wf_ref, o_ref, f_ref, h_ref, *, n_plain):
    j = pl.program_id(1)

    @pl.when(j == 0)
    def _():
        h = (x_ref[...] * (1.0 + sc_ref[...]) + sh_ref[...]).astype(BF16)
        h_ref[...] = h
        f_ref[...] = _dot(h, wf_ref[...])

    acc = _dot(h_ref[...], w_ref[...])

    @pl.when(j < n_plain)
    def _():
        o_ref[...] = acc.astype(o_ref.dtype)

    @pl.when(j >= n_plain)
    def _():
        z = acc + b_ref[...]
        o_ref[...] = (1.0 / (1.0 + jnp.exp(-z))).astype(o_ref.dtype)


def _inproj(x2, sc, sh, w_packed, bias_packed, wf, seq, n_plain_cols):
    t, d = x2.shape
    n = w_packed.shape[1]
    tm = _pick(seq, 512)
    tn = _pick(n_plain_cols, 1024)
    while (n - n_plain_cols) % tn:
        tn //= 2
    per_b = seq // tm
    return pl.pallas_call(
        functools.partial(_inproj_kernel, n_plain=n_plain_cols // tn),
        grid=(t // tm, n // tn),
        in_specs=[pl.BlockSpec((tm, d), lambda i, j: (i, 0)),
                  pl.BlockSpec((None, 1, d), lambda i, j: (i // per_b, 0, 0)),
                  pl.BlockSpec((None, 1, d), lambda i, j: (i // per_b, 0, 0)),
                  pl.BlockSpec((d, tn), lambda i, j: (0, j)),
                  pl.BlockSpec((1, tn), lambda i, j: (0, j)),
                  pl.BlockSpec((d, LANES), lambda i, j: (0, 0))],
        out_specs=[pl.BlockSpec((tm, tn), lambda i, j: (i, j)),
                   pl.BlockSpec((tm, LANES), lambda i, j: (i, 0))],
        out_shape=[jax.ShapeDtypeStruct((t, n), BF16),
                   jax.ShapeDtypeStruct((t, LANES), F32)],
        scratch_shapes=[pltpu.VMEM((tm, d), BF16)],
        compiler_params=_cparams(("parallel", "arbitrary")),
        name="in_proj",
    )(x2, sc, sh, w_packed, bias_packed, wf)


def _cum_kernel(f_ref, b_ref, o_ref, carry_ref, *, per_b):
    i = pl.program_id(0)

    @pl.when(i % per_b == 0)
    def _():
        carry_ref[...] = jnp.zeros_like(carry_ref)

    z = f_ref[...] + b_ref[...]
    lf = jnp.minimum(z, 0.0) - jnp.log(1.0 + jnp.exp(-jnp.abs(z)))
    tb = lf.shape[0]
    row = lax.broadcasted_iota(I32, (tb, tb), 0)
    col = lax.broadcasted_iota(I32, (tb, tb), 1)
    tri = jnp.where(col <= row, 1.0, 0.0).astype(F32)
    cum = jnp.dot(tri, lf, preferred_element_type=F32,
                  precision=lax.Precision.HIGHEST) + carry_ref[...]
    o_ref[...] = cum
    carry_ref[...] = cum[tb - 1:tb, :]


def _forget_cumsum(fa, bf_pad, seq):
    t = fa.shape[0]
    tb = _pick(seq, 512)
    return pl.pallas_call(
        functools.partial(_cum_kernel, per_b=seq // tb),
        grid=(t // tb,),
        in_specs=[pl.BlockSpec((tb, LANES), lambda i: (i, 0)),
                  pl.BlockSpec((1, LANES), lambda i: (0, 0))],
        out_specs=pl.BlockSpec((tb, LANES), lambda i: (i, 0)),
        out_shape=jax.ShapeDtypeStruct((t, LANES), F32),
        scratch_shapes=[pltpu.VMEM((1, LANES), F32)],
        compiler_params=_cparams(("arbitrary",)),
        name="forget_cumsum",
    )(fa, bf_pad)


def _hs(h):
    return slice(h * HEAD_DIM, (h + 1) * HEAD_DIM)


def _fox_kernel(q_ref, k_ref, v_ref, ck_ref, o_ref, *, tq, hp):
    i = pl.program_id(2)
    q0 = pl.multiple_of(i * tq, tq)
    heads = range(hp)
    qs = [q_ref[:, _hs(h)] for h in heads]
    c0s = [ck_ref[h, :, pl.ds(q0, tq)][:, 0:1] for h in heads]

    def scores(h, k0):
        s = _dot_nt(qs[h], k_ref[pl.ds(k0, tq), _hs(h)])
        return s + (c0s[h] - ck_ref[h, :, pl.ds(k0, tq)]) * LOG2E

    def update(h, s, k0, m, l, acc):
        m_new = jnp.maximum(m, jnp.max(s, axis=1, keepdims=True))
        p = jnp.exp2(s - m_new)
        alpha = jnp.exp2(m - m_new)
        l = alpha * l + jnp.sum(p, axis=1, keepdims=True)
        acc = alpha * acc + _dot(p.astype(BF16), v_ref[pl.ds(k0, tq), _hs(h)])
        return m_new, l, acc

    def body(j, carry):
        k0 = pl.multiple_of(j * tq, tq)
        return tuple(update(h, scores(h, k0), k0, *carry[h]) for h in heads)

    init = tuple((jnp.full((tq, 1), NEG_BIG, F32), jnp.zeros((tq, 1), F32),
                  jnp.zeros((tq, HEAD_DIM), F32)) for _ in heads)
    carry = lax.fori_loop(0, i, body, init)
    row = lax.broadcasted_iota(I32, (tq, tq), 0)
    col = lax.broadcasted_iota(I32, (tq, tq), 1)
    for h in heads:
        s = jnp.where(col <= row, scores(h, q0), NEG_BIG)
        _, l, acc = update(h, s, q0, *carry[h])
        o_ref[:, _hs(h)] = (acc / l).astype(o_ref.dtype)


def _fox_attention(proj, cum_rows, bsz, seq, heads, q_blk, k_blk, v_blk):
    tq = _pick(seq, FOX_TQ)
    hp = _pick(heads, FOX_HP)
    nq = seq // tq
    wd = hp * HEAD_DIM
    return pl.pallas_call(
        functools.partial(_fox_kernel, tq=tq, hp=hp),
        grid=(bsz, heads // hp, nq),
        in_specs=[pl.BlockSpec((tq, wd), lambda b, h, i: (b * nq + i, q_blk // hp + h)),
                  pl.BlockSpec((seq, wd), lambda b, h, i: (b, k_blk // hp + h)),
                  pl.BlockSpec((seq, wd), lambda b, h, i: (b, v_blk // hp + h)),
                  pl.BlockSpec((hp, 1, seq), lambda b, h, i: (b * (heads // hp) + h, 0, 0))],
        out_specs=pl.BlockSpec((tq, wd), lambda b, h, i: (b * nq + i, h)),
        out_shape=jax.ShapeDtypeStruct((bsz * seq, heads * HEAD_DIM), BF16),
        compiler_params=_cparams(("parallel", "parallel", "arbitrary")),
        name="fox_attention",
    )(proj, proj, proj, cum_rows)


def _sb_kernel(q_ref, k_ref, v_ref, o_ref, *, tq, tk, hp):
    i = pl.program_id(2)
    q0 = pl.multiple_of(i * tq, tq)
    heads = range(hp)
    n_diag = tq // tk
    r_io = lax.broadcasted_iota(I32, (tk, tk), 0)
    c_io = lax.broadcasted_iota(I32, (tk, tk), 1)
    not_before = jnp.where(r_io >= c_io, 1.0, 0.0).astype(BF16)
    row = lax.broadcasted_iota(I32, (tq, tk), 0)
    col = lax.broadcasted_iota(I32, (tq, tk), 1)
    qs = [q_ref[:, _hs(h)] for h in heads]

    def tile(h, k0):
        z = _dot_nt(qs[h], k_ref[pl.ds(k0, tk), _hs(h)])
        sp = jnp.maximum(z, 0.0) + jnp.log2(1.0 + jnp.exp2(-jnp.abs(z)))
        return z, sp

    runs = [jnp.zeros((tq, 1), F32) for _ in heads]
    accs = [jnp.zeros((tq, HEAD_DIM), F32) for _ in heads]
    for dblk in range(n_diag):
        off = (n_diag - 1 - dblk) * tk
        valid = col + off < row
        k0 = pl.multiple_of(q0 + off, tk)
        for h in heads:
            z, sp = tile(h, k0)
            sp = jnp.where(valid, sp, 0.0)
            suffix = _dot(sp.astype(BF16), not_before)
            w = jnp.where(valid, jnp.exp2(z - suffix - runs[h]), 0.0)
            accs[h] = accs[h] + _dot(w.astype(BF16), v_ref[pl.ds(k0, tk), _hs(h)])
            runs[h] = runs[h] + suffix[:, 0:1]

    def body(n, carry):
        runs, accs = list(carry[0]), list(carry[1])
        for u in range(n_diag):
            k0 = pl.multiple_of(q0 - (n * n_diag + u + 1) * tk, tk)
            for h in heads:
                z, sp = tile(h, k0)
                suffix = _dot(sp.astype(BF16), not_before)
                w = jnp.exp2(z - suffix - runs[h])
                accs[h] = accs[h] + _dot(w.astype(BF16), v_ref[pl.ds(k0, tk), _hs(h)])
                runs[h] = runs[h] + suffix[:, 0:1]
        return tuple(runs), tuple(accs)

    _, accs = lax.fori_loop(0, i, body, (tuple(runs), tuple(accs)))
    for h in heads:
        o_ref[:, _hs(h)] = accs[h].astype(o_ref.dtype)


def _sb_attention(proj, bsz, seq, heads, q_blk, k_blk, v_blk):
    tq = _pick(seq, SB_TQ)
    tk = _pick(tq, SB_TK)
    hp = _pick(heads, SB_HP)
    nq = seq // tq
    wd = hp * HEAD_DIM
    return pl.pallas_call(
        functools.partial(_sb_kernel, tq=tq, tk=tk, hp=hp),
        grid=(bsz, heads // hp, nq),
        in_specs=[pl.BlockSpec((tq, wd), lambda b, h, i: (b * nq + i, q_blk // hp + h)),
                  pl.BlockSpec((seq, wd), lambda b, h, i: (b, k_blk // hp + h)),
                  pl.BlockSpec((seq, wd), lambda b, h, i: (b, v_blk // hp + h))],
        out_specs=pl.BlockSpec((tq, wd), lambda b, h, i: (b * nq + i, h)),
        out_shape=jax.ShapeDtypeStruct((bsz * seq, heads * HEAD_DIM), BF16),
        compiler_params=_cparams(("parallel", "parallel", "arbitrary")),
        name="sb_attention",
    )(proj, proj, proj)


def _merge_kernel(of_ref, os_ref, wa_ref, wb_ref, gf_ref, gs_ref, o_ref):
    yf = _dot(of_ref[...], wa_ref[...])
    ys = _dot(os_ref[...], wb_ref[...])
    o_ref[...] = (gf_ref[...].astype(F32) * yf + gs_ref[...].astype(F32) * ys).astype(o_ref.dtype)


def _merge(o_fox, o_sb, wa, wb, proj, g_col0, d):
    t = o_fox.shape[0]
    tm = _pick(t, 512)
    tn = _pick(d, 1024)
    gf0 = g_col0 // tn
    gs0 = (g_col0 + d) // tn
    return pl.pallas_call(
        _merge_kernel,
        grid=(t // tm, d // tn),
        in_specs=[pl.BlockSpec((tm, o_fox.shape[1]), lambda i, j: (i, 0)),
                  pl.BlockSpec((tm, o_sb.shape[1]), lambda i, j: (i, 0)),
                  pl.BlockSpec((wa.shape[0], tn), lambda i, j: (0, j)),
                  pl.BlockSpec((wb.shape[0], tn), lambda i, j: (0, j)),
                  pl.BlockSpec((tm, tn), lambda i, j: (i, gf0 + j)),
                  pl.BlockSpec((tm, tn), lambda i, j: (i, gs0 + j))],
        out_specs=pl.BlockSpec((tm, tn), lambda i, j: (i, j)),
        out_shape=jax.ShapeDtypeStruct((t, d), BF16),
        compiler_params=_cparams(("parallel", "arbitrary")),
        name="branch_merge",
    )(o_fox, o_sb, wa, wb, proj, proj)


def _layer_norm(v, g, b):
    mu = jnp.mean(v, axis=-1, keepdims=True)
    c = v - mu
    var = jnp.mean(c * c, axis=-1, keepdims=True)
    return c * lax.rsqrt(var + LN_EPS) * g + b


def _outln_kernel(m_ref, w_ref, x_ref, g1_ref, lg_ref, lb_ref, sc_ref, sh_ref,
                  x1_ref, h2_ref, y_ref, *, tn, alpha):
    j = pl.program_id(1)
    y_ref[:, pl.ds(pl.multiple_of(j * tn, tn), tn)] = _dot(m_ref[...], w_ref[...])

    @pl.when(j == pl.num_programs(1) - 1)
    def _():
        v = alpha * x_ref[...] + g1_ref[...] * y_ref[...]
        x1 = _layer_norm(v, lg_ref[...], lb_ref[...])
        x1_ref[...] = x1
        h2_ref[...] = x1 * (1.0 + sc_ref[...]) + sh_ref[...]


def _out_ln(merged, w_out, x2, g1, ln_g, ln_b, sc2, sh2, seq, alpha):
    t, d = x2.shape
    tm = _pick(seq, 256)
    tn = _pick(d, 512)
    per_b = seq // tm
    row = lambda i, j: (i, 0)
    per_batch = lambda i, j: (i // per_b, 0, 0)
    const = lambda i, j: (0, 0)
    return pl.pallas_call(
        functools.partial(_outln_kernel, tn=tn, alpha=alpha),
        grid=(t // tm, d // tn),
        in_specs=[pl.BlockSpec((tm, d), row),
                  pl.BlockSpec((d, tn), lambda i, j: (0, j)),
                  pl.BlockSpec((tm, d), row),
                  pl.BlockSpec((None, 1, d), per_batch),
                  pl.BlockSpec((1, d), const),
                  pl.BlockSpec((1, d), const),
                  pl.BlockSpec((None, 1, d), per_batch),
                  pl.BlockSpec((None, 1, d), per_batch)],
        out_specs=[pl.BlockSpec((tm, d), row), pl.BlockSpec((tm, d), row)],
        out_shape=[jax.ShapeDtypeStruct((t, d), F32), jax.ShapeDtypeStruct((t, d), F32)],
        scratch_shapes=[pltpu.VMEM((tm, d), F32)],
        compiler_params=_cparams(("parallel", "arbitrary")),
        name="out_proj_ln1",
    )(merged, w_out, x2, g1, ln_g, ln_b, sc2, sh2)


def _route_kernel(h_ref, wr_ref, rb_ref, idx_ref, rank_ref, w_ref, cnt_ref, carry_ref, *, n_exp):
    i = pl.program_id(0)

    @pl.when(i == 0)
    def _():
        carry_ref[...] = jnp.zeros_like(carry_ref)

    tb = h_ref.shape[0]
    gsz = n_exp // N_GROUPS
    logits = _dot_nt(wr_ref[...], h_ref[...], precision=lax.Precision.HIGHEST)
    scores = 1.0 / (1.0 + jnp.exp(-logits))
    choice = scores + rb_ref[...][:, 0:1]
    neg = -jnp.inf

    g = choice.reshape(N_GROUPS, gsz, tb)
    io = lax.broadcasted_iota(I32, g.shape, 1)
    m1 = jnp.max(g, axis=1, keepdims=True)
    first = jnp.min(jnp.where(g == m1, io, gsz), axis=1, keepdims=True)
    m2 = jnp.max(jnp.where(io == first, neg, g), axis=1, keepdims=True)
    gs = (m1 + m2).reshape(N_GROUPS, tb)

    gio = lax.broadcasted_iota(I32, gs.shape, 0)
    gsel = jnp.zeros(gs.shape, F32)
    for _ in range(TOPK_GROUPS):
        m = jnp.max(gs, axis=0, keepdims=True)
        f = jnp.min(jnp.where(gs == m, gio, N_GROUPS), axis=0, keepdims=True)
        hit = gio == f
        gsel = jnp.where(hit, 1.0, gsel)
        gs = jnp.where(hit, neg, gs)
    emask = jnp.broadcast_to(gsel.reshape(N_GROUPS, 1, tb), (N_GROUPS, gsz, tb)).reshape(n_exp, tb)

    v = jnp.where(emask > 0.5, choice, neg)
    eio = lax.broadcasted_iota(I32, v.shape, 0)
    sel = jnp.zeros(v.shape, F32)
    idx_rows, w_rows = [], []
    for _ in range(TOP_K):
        m = jnp.max(v, axis=0, keepdims=True)
        f = jnp.min(jnp.where(v == m, eio, n_exp), axis=0, keepdims=True)
        hit = eio == f
        idx_rows.append(f)
        w_rows.append(jnp.sum(jnp.where(hit, scores, 0.0), axis=0, keepdims=True))
        sel = jnp.where(hit, 1.0, sel)
        v = jnp.where(hit, neg, v)
    wsum = w_rows[0]
    for r in range(1, TOP_K):
        wsum = wsum + w_rows[r]
    inv = 1.0 / (wsum + 1e-20)

    r_io = lax.broadcasted_iota(I32, (tb, tb), 0)
    c_io = lax.broadcasted_iota(I32, (tb, tb), 1)
    before = jnp.where(r_io < c_io, 1.0, 0.0).astype(BF16)
    excl = _dot(sel.astype(BF16), before) + carry_ref[...][:, 0:1]
    for r in range(TOP_K):
        hit = eio == idx_rows[r]
        rank = jnp.sum(jnp.where(hit, excl, 0.0), axis=0, keepdims=True)
        idx_ref[r:r + 1, :] = idx_rows[r]
        rank_ref[r:r + 1, :] = rank.astype(I32)
        w_ref[r:r + 1, :] = w_rows[r] * inv * ROUTED_SCALE
    total = carry_ref[...] + jnp.sum(sel, axis=1, keepdims=True)
    carry_ref[...] = total
    cnt_ref[...] = total.astype(I32)


def _route(h2, wr_t, rb):
    t, d = h2.shape
    n_exp = wr_t.shape[0]
    tb = _pick(t, 512)
    blk = pl.BlockSpec((TOP_K, tb), lambda i: (0, i))
    return pl.pallas_call(
        functools.partial(_route_kernel, n_exp=n_exp),
        grid=(t // tb,),
        in_specs=[pl.BlockSpec((tb, d), lambda i: (i, 0)),
                  pl.BlockSpec((n_exp, d), lambda i: (0, 0)),
                  pl.BlockSpec((n_exp, LANES), lambda i: (0, 0))],
        out_specs=[blk, blk, blk, pl.BlockSpec((n_exp, LANES), lambda i: (0, 0))],
        out_shape=[jax.ShapeDtypeStruct((TOP_K, t), I32),
                   jax.ShapeDtypeStruct((TOP_K, t), I32),
                   jax.ShapeDtypeStruct((TOP_K, t), F32),
                   jax.ShapeDtypeStruct((n_exp, LANES), I32)],
        scratch_shapes=[pltpu.VMEM((n_exp, LANES), F32)],
        compiler_params=_cparams(("arbitrary",)),
        name="router_topk",
    )(h2, wr_t, rb)


def _dest_kernel(idx_ref, rank_ref, cnt_ref, dest_ref, be_ref, meta_ref, *, n_exp, tm, nbp):
    cnt = cnt_ref[...]
    nblk = jnp.right_shift(cnt + (tm - 1), tm.bit_length() - 1)
    r_io = lax.broadcasted_iota(I32, (n_exp, n_exp), 0)
    c_io = lax.broadcasted_iota(I32, (n_exp, n_exp), 1)
    below = jnp.where(c_io < r_io, 1.0, 0.0).astype(F32)
    start_blk = jnp.dot(below, nblk.astype(F32), preferred_element_type=F32,
                        precision=lax.Precision.HIGHEST).astype(I32)
    end_blk = start_blk + nblk
    start_row = start_blk * tm
    idx = idx_ref[...]
    dest = rank_ref[...]
    for e in range(n_exp):
        dest = dest + jnp.where(idx == e, start_row[e:e + 1, 0:1], 0)
    dest_ref[...] = dest
    b_io = lax.broadcasted_iota(I32, (n_exp, nbp), 1)
    done = jnp.sum(jnp.where(b_io >= end_blk[:, 0:1], 1.0, 0.0), axis=0, keepdims=True)
    be_ref[...] = jnp.minimum(done.astype(I32), n_exp - 1)
    meta_ref[...] = end_blk[n_exp - 1:n_exp, :]


def _dest(idx, rank, counts, n_exp, tm, n_blocks):
    assert tm & (tm - 1) == 0
    t = idx.shape[1]
    tb = _pick(t, 2048)
    nbp = -(-n_blocks // LANES) * LANES
    blk = pl.BlockSpec((TOP_K, tb), lambda i: (0, i))
    return pl.pallas_call(
        functools.partial(_dest_kernel, n_exp=n_exp, tm=tm, nbp=nbp),
        grid=(t // tb,),
        in_specs=[blk, blk, pl.BlockSpec((n_exp, LANES), lambda i: (0, 0))],
        out_specs=[blk, pl.BlockSpec((1, nbp), lambda i: (0, 0)),
                   pl.BlockSpec((1, LANES), lambda i: (0, 0))],
        out_shape=[jax.ShapeDtypeStruct((TOP_K, t), I32),
                   jax.ShapeDtypeStruct((1, nbp), I32),
                   jax.ShapeDtypeStruct((1, LANES), I32)],
        compiler_params=_cparams(("arbitrary",)),
        name="dispatch_rows",
    )(idx, rank, counts)


def _invert_kernel(dest_ref, tok_ref, *, chunk, n_tok, n_rows):
    c = pl.program_id(0)

    @pl.when(c == 0)
    def _():
        def zero(r, _):
            tok_ref[r] = 0
            return 0
        lax.fori_loop(0, n_rows, zero, 0, unroll=DMA_UNROLL)

    tok0 = (c * chunk) % n_tok

    def place(n, _):
        tok_ref[dest_ref[n]] = tok0 + n
        return 0
    lax.fori_loop(0, chunk, place, 0, unroll=DMA_UNROLL)


def _invert(dest_flat, n_tok, n_rows):
    n_assign = dest_flat.shape[0]
    chunk = _pick(n_tok, 8192)
    return pl.pallas_call(
        functools.partial(_invert_kernel, chunk=chunk, n_tok=n_tok, n_rows=n_rows),
        grid=(n_assign // chunk,),
        in_specs=[pl.BlockSpec((chunk,), lambda c: (c,), memory_space=pltpu.SMEM)],
        out_specs=pl.BlockSpec(memory_space=pltpu.SMEM),
        out_shape=jax.ShapeDtypeStruct((n_rows,), I32),
        compiler_params=_cparams(("arbitrary",)),
        name="dispatch_tokens",
    )(dest_flat)


def _moe_kernel(tok_ref, be_ref, meta_ref, h_hbm, wg_ref, wu_ref, wd_ref, y_ref, xbuf, xb_ref, sem,
                *, tm):
    b = pl.program_id(0)
    used = meta_ref[0]

    def gather(blk, slot):
        base = blk * tm

        def issue(r, _):
            tok = tok_ref[base + r]
            pltpu.make_async_copy(h_hbm.at[pl.ds(tok, 1)], xbuf.at[slot, pl.ds(r, 1)],
                                  sem.at[slot]).start()
            return 0
        lax.fori_loop(0, tm, issue, 0, unroll=DMA_UNROLL)

    def wait_rows(slot):
        pltpu.make_async_copy(h_hbm.at[pl.ds(0, tm)], xbuf.at[slot], sem.at[slot]).wait()

    @pl.when(b == 0)
    def _():
        gather(0, 0)

    @pl.when(b <= used)
    def _():
        wait_rows(b % 2)

    @pl.when(b < used)
    def _():
        xb_ref[...] = xbuf[b % 2].astype(BF16)
        nxt = (b + 1) % 2
        for r in range(tm):
            tok = tok_ref[(b + 1) * tm + r]
            pltpu.make_async_copy(h_hbm.at[pl.ds(tok, 1)], xbuf.at[nxt, pl.ds(r, 1)],
                                  sem.at[nxt]).start()
        x = xb_ref[...]
        g = _dot(x, wg_ref[...])
        u = _dot(x, wu_ref[...])
        a = (g * (1.0 / (1.0 + jnp.exp(-g))) * u).astype(BF16)
        y_ref[...] = _dot(a, wd_ref[...])

    @pl.when(b >= used)
    def _():
        y_ref[...] = jnp.zeros_like(y_ref)


def _moe(tok_buf, block_e, meta, h2, wg, wu, wd, tm):
    n_rows = tok_buf.shape[0]
    d = h2.shape[1]
    f = wg.shape[2]
    grid_spec = pltpu.PrefetchScalarGridSpec(
        num_scalar_prefetch=3,
        grid=(n_rows // tm,),
        in_specs=[pl.BlockSpec(memory_space=pl.ANY),
                  pl.BlockSpec((None, d, f), lambda b, tok, be, meta: (be[b], 0, 0)),
                  pl.BlockSpec((None, d, f), lambda b, tok, be, meta: (be[b], 0, 0)),
                  pl.BlockSpec((None, f, d), lambda b, tok, be, meta: (be[b], 0, 0))],
        out_specs=pl.BlockSpec((tm, d), lambda b, tok, be, meta: (b, 0)),
        scratch_shapes=[pltpu.VMEM((2, tm, d), F32), pltpu.VMEM((tm, d), BF16),
                        pltpu.SemaphoreType.DMA((2,))],
    )
    return pl.pallas_call(
        functools.partial(_moe_kernel, tm=tm),
        grid_spec=grid_spec,
        out_shape=jax.ShapeDtypeStruct((n_rows, d), F32),
        compiler_params=_cparams(("arbitrary",), disable_bounds_checks=True),
        name="routed_experts",
    )(tok_buf, block_e, meta, h2, wg, wu, wd)


def _shared_kernel(h_ref, wg_ref, wu_ref, wd_ref, o_ref):
    x = h_ref[...].astype(BF16)
    g = _dot(x, wg_ref[...])
    u = _dot(x, wu_ref[...])
    a = (g * (1.0 / (1.0 + jnp.exp(-g))) * u).astype(BF16)
    o_ref[...] = _dot(a, wd_ref[...]).astype(o_ref.dtype)


def _shared(h2, wg, wu, wd):
    t, d = h2.shape
    f = wg.shape[1]
    tm = _pick(t, 256)
    return pl.pallas_call(
        _shared_kernel,
        grid=(t // tm,),
        in_specs=[pl.BlockSpec((tm, d), lambda i: (i, 0)),
                  pl.BlockSpec((d, f), lambda i: (0, 0)),
                  pl.BlockSpec((d, f), lambda i: (0, 0)),
                  pl.BlockSpec((f, d), lambda i: (0, 0))],
        out_specs=pl.BlockSpec((tm, d), lambda i: (i, 0)),
        out_shape=jax.ShapeDtypeStruct((t, d), BF16),
        compiler_params=_cparams(("parallel",)),
        name="shared_expert",
    )(h2, wg, wu, wd)


def _final_kernel(dest_ref, y_hbm, w_ref, x1_ref, s_ref, g2_ref, lg_ref, lb_ref, o_ref, ybuf, sem,
                  *, tb, n_tok, alpha):
    i = pl.program_id(0)
    nsteps = pl.num_programs(0)

    def gather(blk, slot):
        for r in range(TOP_K):
            base = r * n_tok + blk * tb

            def issue(n, _):
                row = dest_ref[base + n]
                pltpu.make_async_copy(y_hbm.at[pl.ds(row, 1)], ybuf.at[slot, pl.ds(r * tb + n, 1)],
                                      sem.at[slot]).start()
                return 0
            lax.fori_loop(0, tb, issue, 0, unroll=DMA_UNROLL)

    @pl.when(i == 0)
    def _():
        gather(0, 0)

    @pl.when(i + 1 < nsteps)
    def _():
        gather(i + 1, (i + 1) % 2)

    slot = i % 2
    pltpu.make_async_copy(y_hbm.at[pl.ds(0, TOP_K * tb)], ybuf.at[slot], sem.at[slot]).wait()

    w = w_ref[...]
    routed = w[:, 0:1] * ybuf[slot, 0:tb]
    for r in range(1, TOP_K):
        routed = routed + w[:, r:r + 1] * ybuf[slot, r * tb:(r + 1) * tb]
    ffn = routed + s_ref[...].astype(F32)
    v = alpha * x1_ref[...] + g2_ref[...] * ffn
    o_ref[...] = _layer_norm(v, lg_ref[...], lb_ref[...])


def _final(dest, ys, w_tok, x1, shared, g2, ln_g, ln_b, seq, alpha):
    t, d = x1.shape
    tb = _pick(seq, COMBINE_TB)
    per_b = seq // tb
    grid_spec = pltpu.PrefetchScalarGridSpec(
        num_scalar_prefetch=1,
        grid=(t // tb,),
        in_specs=[pl.BlockSpec(memory_space=pl.ANY),
                  pl.BlockSpec((tb, TOP_K), lambda i, dest: (i, 0)),
                  pl.BlockSpec((tb, d), lambda i, dest: (i, 0)),
                  pl.BlockSpec((tb, d), lambda i, dest: (i, 0)),
                  pl.BlockSpec((None, 1, d), lambda i, dest: (i // per_b, 0, 0)),
                  pl.BlockSpec((1, d), lambda i, dest: (0, 0)),
                  pl.BlockSpec((1, d), lambda i, dest: (0, 0))],
        out_specs=pl.BlockSpec((tb, d), lambda i, dest: (i, 0)),
        scratch_shapes=[pltpu.VMEM((2, TOP_K * tb, d), F32), pltpu.SemaphoreType.DMA((2,))],
    )
    return pl.pallas_call(
        functools.partial(_final_kernel, tb=tb, n_tok=t, alpha=alpha),
        grid_spec=grid_spec,
        out_shape=jax.ShapeDtypeStruct((t, d), F32),
        compiler_params=_cparams(("arbitrary",), disable_bounds_checks=True),
        name="combine_ln2",
    )(dest, ys, w_tok, x1, shared, g2, ln_g, ln_b)


def _layer(x2, c, bsz, seq, w_ada, b_ada, w_in, b_forget, b_gate, w_branch_fox, w_branch_sb, w_out,
           ln1_g, ln1_b, w_router, router_bias, w_exp_gate, w_exp_up, w_exp_down,
           w_sh_gate, w_sh_up, w_sh_down, ln2_g, ln2_b, alpha):
    t, d = x2.shape
    hf = b_forget.shape[0]
    fw = w_branch_fox.shape[0]
    sw = w_branch_sb.shape[0]
    hs = sw // HEAD_DIM
    n_exp = w_router.shape[1]

    mod = _ada(c, w_ada, b_ada)
    sh1, sc1, g1, sh2, sc2, g2 = [m.reshape(bsz, 1, d) for m in jnp.split(mod, 6, axis=-1)]

    f0 = 3 * fw
    qkv_cols = 3 * fw + 3 * sw
    qs = HEAD_DIM ** -0.5 * LOG2E
    w_packed = jnp.concatenate(
        [w_in[:, :fw] * qs, w_in[:, fw:f0], w_in[:, f0 + hf:f0 + hf + sw] * qs, w_in[:, f0 + hf + sw:]],
        axis=1).astype(BF16)
    bias_packed = jnp.concatenate([jnp.zeros((qkv_cols,), F32), b_gate]).reshape(1, -1)
    wf = jnp.zeros((d, LANES), F32).at[:, :hf].set(w_in[:, f0:f0 + hf]).astype(BF16)
    bf_pad = jnp.zeros((1, LANES), F32).at[0, :hf].set(b_forget)

    proj, fa = _inproj(x2, sc1, sh1, w_packed, bias_packed, wf, seq, qkv_cols)
    cum = _forget_cumsum(fa, bf_pad, seq)
    cum_rows = cum[:, :hf].reshape(bsz, seq, hf).transpose(0, 2, 1).reshape(bsz * hf, 1, seq)

    o_fox = _fox_attention(proj, cum_rows, bsz, seq, hf, 0, hf, 2 * hf)
    o_sb = _sb_attention(proj, bsz, seq, hs, 3 * hf, 3 * hf + hs, 3 * hf + 2 * hs)
    merged = _merge(o_fox, o_sb, w_branch_fox.astype(BF16), w_branch_sb.astype(BF16), proj,
                    qkv_cols, d)
    x1, h2 = _out_ln(merged, w_out.astype(BF16), x2, g1, ln1_g.reshape(1, d), ln1_b.reshape(1, d),
                     sc2, sh2, seq, alpha)

    rb = jnp.broadcast_to(router_bias.reshape(n_exp, 1), (n_exp, LANES))
    idx, rank, w_top, counts = _route(h2, w_router.T, rb)
    tm = MOE_TM
    n_rows = t * TOP_K + n_exp * tm
    dest2, be2, meta2 = _dest(idx, rank, counts, n_exp, tm, n_rows // tm)
    dest, block_e, meta = dest2.reshape(-1), be2[0, :n_rows // tm], meta2[0, :1]
    tok_buf = _invert(dest, t, n_rows)
    ys = _moe(tok_buf, block_e, meta, h2, w_exp_gate.astype(BF16), w_exp_up.astype(BF16),
              w_exp_down.astype(BF16), tm)
    shared = _shared(h2, w_sh_gate.astype(BF16), w_sh_up.astype(BF16), w_sh_down.astype(BF16))
    return _final(dest, ys, w_top.T, x1, shared, g2, ln2_g.reshape(1, d), ln2_b.reshape(1, d),
                  seq, alpha)


def kernel(x, c, w_ada, b_ada, w_in, b_forget, b_gate, w_branch_fox, w_branch_sb, w_out, ln1_g, ln1_b, w_router, router_bias, w_exp_gate, w_exp_up, w_exp_down, w_sh_gate, w_sh_up, w_sh_down, ln2_g, ln2_b):
    bsz, seq, d = x.shape
    depth = w_ada.shape[0]
    alpha = (2 * depth) ** 0.25
    x2 = x.reshape(bsz * seq, d)
    for l in range(depth):
        x2 = _layer(x2, c, bsz, seq, w_ada[l], b_ada[l], w_in[l], b_forget[l], b_gate[l],
                    w_branch_fox[l], w_branch_sb[l], w_out[l], ln1_g[l], ln1_b[l], w_router[l],
                    router_bias[l], w_exp_gate[l], w_exp_up[l], w_exp_down[l], w_sh_gate[l],
                    w_sh_up[l], w_sh_down[l], ln2_g[l], ln2_b[l], alpha)
    return x2.reshape(bsz, seq, d)
```

```python
import functools

import jax
import jax.numpy as jnp
from jax import lax
from jax.experimental import pallas as pl
from jax.experimental.pallas import tpu as pltpu

F32 = jnp.float32
BF16 = jnp.bfloat16
I32 = jnp.int32

HEAD_DIM = 128
LANES = 128
N_GROUPS = 8
TOPK_GROUPS = 4
TOP_K = 8
ROUTED_SCALE = 2.5
LN_EPS = 1e-5
VMEM_LIMIT = 56 * 1024 * 1024
NEG_BIG = -1e30
LOG2E = 1.4426950408889634
FOX_TQ, FOX_TK, FOX_HP = 512, 512, 2
SB_TQ, SB_TK, SB_HP = 512, 256, 4
MOE_TM = 256
MOE_SLOTS = 3
COMBINE_TB = 128
DMA_UNROLL = 8


def _cparams(sem, **kw):
    return pltpu.CompilerParams(dimension_semantics=sem, vmem_limit_bytes=VMEM_LIMIT, **kw)


def _pick(n, pref):
    t = min(n, pref)
    while n % t:
        t //= 2
    return t


def _dot(a, b):
    return jnp.dot(a, b, preferred_element_type=F32)


def _dot_nt(a, b, precision=None):
    return lax.dot_general(a, b, (((1,), (1,)), ((), ())), preferred_element_type=F32,
                           precision=precision)


def _ada_kernel(c_ref, w_ref, b_ref, o_ref):
    c = c_ref[...]
    s = c * (1.0 / (1.0 + jnp.exp(-c)))
    o_ref[...] = jnp.dot(s, w_ref[...], preferred_element_type=F32,
                         precision=lax.Precision.HIGHEST) + b_ref[...]


def _ada(c, w_ada, b_ada):
    bsz, d = c.shape
    n = w_ada.shape[1]
    rows = 8
    c_pad = jnp.zeros((rows, d), F32).at[:bsz].set(c)
    tn = _pick(n, 512)
    out = pl.pallas_call(
        _ada_kernel,
        grid=(n // tn,),
        in_specs=[pl.BlockSpec((rows, d), lambda j: (0, 0)),
                  pl.BlockSpec((d, tn), lambda j: (0, j)),
                  pl.BlockSpec((1, tn), lambda j: (0, j))],
        out_specs=pl.BlockSpec((rows, tn), lambda j: (0, j)),
        out_shape=jax.ShapeDtypeStruct((rows, n), F32),
        compiler_params=_cparams(("parallel",)),
        name="ada_mod",
    )(c_pad, w_ada, b_ada.reshape(1, n))
    return out[:bsz]


def _inproj_kernel(x_ref, sc_ref, sh_ref, w_ref, b_ref, wf_ref, o_ref, f_ref, h_ref, *, n_plain):
    j = pl.program_id(1)

    @pl.when(j == 0)
    def _():
        h = (x_ref[...] * (1.0 + sc_ref[...]) + sh_ref[...]).astype(BF16)
        h_ref[...] = h
        f_ref[...] = _dot(h, wf_ref[...])

    acc = _dot(h_ref[...], w_ref[...])

    @pl.when(j < n_plain)
    def _():
        o_ref[...] = acc.astype(o_ref.dtype)

    @pl.when(j >= n_plain)
    def _():
        z = acc + b_ref[...]
        o_ref[...] = (1.0 / (1.0 + jnp.exp(-z))).astype(o_ref.dtype)


def _inproj(x2, sc, sh, w_packed, bias_packed, wf, seq, n_plain_cols):
    t, d = x2.shape
    n = w_packed.shape[1]
    tm = _pick(seq, 512)
    tn = _pick(n_plain_cols, 1024)
    while (n - n_plain_cols) % tn:
        tn //= 2
    per_b = seq // tm
    return pl.pallas_call(
        functools.partial(_inproj_kernel, n_plain=n_plain_cols // tn),
        grid=(t // tm, n // tn),
        in_specs=[pl.BlockSpec((tm, d), lambda i, j: (i, 0)),
                  pl.BlockSpec((None, 1, d), lambda i, j: (i // per_b, 0, 0)),
                  pl.BlockSpec((None, 1, d), lambda i, j: (i // per_b, 0, 0)),
                  pl.BlockSpec((d, tn), lambda i, j: (0, j)),
                  pl.BlockSpec((1, tn), lambda i, j: (0, j)),
                  pl.BlockSpec((d, LANES), lambda i, j: (0, 0))],
        out_specs=[pl.BlockSpec((tm, tn), lambda i, j: (i, j)),
                   pl.BlockSpec((tm, LANES), lambda i, j: (i, 0))],
        out_shape=[jax.ShapeDtypeStruct((t, n), BF16),
                   jax.ShapeDtypeStruct((t, LANES), F32)],
        scratch_shapes=[pltpu.VMEM((tm, d), BF16)],
        compiler_params=_cparams(("parallel", "arbitrary")),
        name="in_proj",
    )(x2, sc, sh, w_packed, bias_packed, wf)


def _cum_kernel(f_ref, b_ref, o_ref, carry_ref, *, per_b):
    i = pl.program_id(0)

    @pl.when(i % per_b == 0)
    def _():
        carry_ref[...] = jnp.zeros_like(carry_ref)

    z = f_ref[...] + b_ref[...]
    lf = jnp.minimum(z, 0.0) - jnp.log(1.0 + jnp.exp(-jnp.abs(z)))
    tb = lf.shape[0]
    row = lax.broadcasted_iota(I32, (tb, tb), 0)
    col = lax.broadcasted_iota(I32, (tb, tb), 1)
    tri = jnp.where(col <= row, 1.0, 0.0).astype(F32)
    cum = jnp.dot(tri, lf, preferred_element_type=F32,
                  precision=lax.Precision.HIGHEST) + carry_ref[...]
    o_ref[...] = cum
    carry_ref[...] = cum[tb - 1:tb, :]


def _forget_cumsum(fa, bf_pad, seq):
    t = fa.shape[0]
    tb = _pick(seq, 512)
    return pl.pallas_call(
        functools.partial(_cum_kernel, per_b=seq // tb),
        grid=(t // tb,),
        in_specs=[pl.BlockSpec((tb, LANES), lambda i: (i, 0)),
                  pl.BlockSpec((1, LANES), lambda i: (0, 0))],
        out_specs=pl.BlockSpec((tb, LANES), lambda i: (i, 0)),
        out_shape=jax.ShapeDtypeStruct((t, LANES), F32),
        scratch_shapes=[pltpu.VMEM((1, LANES), F32)],
        compiler_params=_cparams(("arbitrary",)),
        name="forget_cumsum",
    )(fa, bf_pad)


def _hs(h):
    return slice(h * HEAD_DIM, (h + 1) * HEAD_DIM)


def _fox_kernel(q_ref, k_ref, v_ref, ck_ref, o_ref, *, tq, tk, hp):
    i = pl.program_id(2)
    q0 = pl.multiple_of(i * tq, tq)
    heads = range(hp)
    n_diag = tq // tk
    qs = [q_ref[:, _hs(h)] for h in heads]
    c0s = [ck_ref[h, :, pl.ds(q0, tk)][:, 0:1] for h in heads]

    def scores(h, k0):
        s = _dot_nt(qs[h], k_ref[pl.ds(k0, tk), _hs(h)])
        return s + (c0s[h] - ck_ref[h, :, pl.ds(k0, tk)]) * LOG2E

    def update(h, s, k0, m, l, acc):
        m_new = jnp.maximum(m, jnp.max(s, axis=1, keepdims=True))
        p = jnp.exp2(s - m_new)
        alpha = jnp.exp2(m - m_new)
        l = alpha * l + jnp.sum(p, axis=1, keepdims=True)
        acc = alpha * acc + _dot(p.astype(BF16), v_ref[pl.ds(k0, tk), _hs(h)])
        return m_new, l, acc

    def body(j, carry):
        carry = list(carry)
        for u in range(n_diag):
            k0 = pl.multiple_of((j * n_diag + u) * tk, tk)
            for h in heads:
                carry[h] = update(h, scores(h, k0), k0, *carry[h])
        return tuple(carry)

    init = tuple((jnp.full((tq, 1), NEG_BIG, F32), jnp.zeros((tq, 1), F32),
                  jnp.zeros((tq, HEAD_DIM), F32)) for _ in heads)
    carry = list(lax.fori_loop(0, i, body, init))
    row = lax.broadcasted_iota(I32, (tq, tk), 0)
    col = lax.broadcasted_iota(I32, (tq, tk), 1)
    for dblk in range(n_diag):
        off = dblk * tk
        k0 = pl.multiple_of(q0 + off, tk)
        for h in heads:
            s = jnp.where(col + off <= row, scores(h, k0), NEG_BIG)
            carry[h] = update(h, s, k0, *carry[h])
    for h in heads:
        _, l, acc = carry[h]
        o_ref[:, _hs(h)] = (acc / l).astype(o_ref.dtype)


def _fox_attention(proj, cum_rows, bsz, seq, heads, q_blk, k_blk, v_blk):
    tq = _pick(seq, FOX_TQ)
    tk = _pick(tq, FOX_TK)
    hp = _pick(heads, FOX_HP)
    nq = seq // tq
    wd = hp * HEAD_DIM
    return pl.pallas_call(
        functools.partial(_fox_kernel, tq=tq, tk=tk, hp=hp),
        grid=(bsz, heads // hp, nq),
        in_specs=[pl.BlockSpec((tq, wd), lambda b, h, i: (b * nq + i, q_blk // hp + h)),
                  pl.BlockSpec((seq, wd), lambda b, h, i: (b, k_blk // hp + h)),
                  pl.BlockSpec((seq, wd), lambda b, h, i: (b, v_blk // hp + h)),
                  pl.BlockSpec((hp, 1, seq), lambda b, h, i: (b * (heads // hp) + h, 0, 0))],
        out_specs=pl.BlockSpec((tq, wd), lambda b, h, i: (b * nq + i, h)),
        out_shape=jax.ShapeDtypeStruct((bsz * seq, heads * HEAD_DIM), BF16),
        compiler_params=_cparams(("parallel", "parallel", "arbitrary")),
        name="fox_attention",
    )(proj, proj, proj, cum_rows)


def _sb_kernel(q_ref, k_ref, v_ref, o_ref, *, tq, tk, hp):
    i = pl.program_id(2)
    q0 = pl.multiple_of(i * tq, tq)
    heads = range(hp)
    n_diag = tq // tk
    r_io = lax.broadcasted_iota(I32, (tk, tk), 0)
    c_io = lax.broadcasted_iota(I32, (tk, tk), 1)
    not_before = jnp.where(r_io >= c_io, 1.0, 0.0).astype(BF16)
    row = lax.broadcasted_iota(I32, (tq, tk), 0)
    col = lax.broadcasted_iota(I32, (tq, tk), 1)
    qs = [q_ref[:, _hs(h)] for h in heads]

    def tile(h, k0):
        z = _dot_nt(qs[h], k_ref[pl.ds(k0, tk), _hs(h)])
        sp = jnp.maximum(z, 0.0) + jnp.log2(1.0 + jnp.exp2(-jnp.abs(z)))
        return z, sp

    runs = [jnp.zeros((tq, 1), F32) for _ in heads]
    accs = [jnp.zeros((tq, HEAD_DIM), F32) for _ in heads]
    for dblk in range(n_diag):
        off = (n_diag - 1 - dblk) * tk
        valid = col + off < row
        k0 = pl.multiple_of(q0 + off, tk)
        for h in heads:
            z, sp = tile(h, k0)
            sp = jnp.where(valid, sp, 0.0)
            suffix = _dot(sp.astype(BF16), not_before)
            w = jnp.where(valid, jnp.exp2(z - suffix - runs[h]), 0.0)
            accs[h] = accs[h] + _dot(w.astype(BF16), v_ref[pl.ds(k0, tk), _hs(h)])
            runs[h] = runs[h] + suffix[:, 0:1]

    def body(n, carry):
        runs, accs = list(carry[0]), list(carry[1])
        for u in range(n_diag):
            k0 = pl.multiple_of(q0 - (n * n_diag + u + 1) * tk, tk)
            for h in heads:
                z, sp = tile(h, k0)
                suffix = _dot(sp.astype(BF16), not_before)
                w = jnp.exp2(z - suffix - runs[h])
                accs[h] = accs[h] + _dot(w.astype(BF16), v_ref[pl.ds(k0, tk), _hs(h)])
                runs[h] = runs[h] + suffix[:, 0:1]
        return tuple(runs), tuple(accs)

    _, accs = lax.fori_loop(0, i, body, (tuple(runs), tuple(accs)))
    for h in heads:
        o_ref[:, _hs(h)] = accs[h].astype(o_ref.dtype)


def _sb_attention(proj, bsz, seq, heads, q_blk, k_blk, v_blk):
    tq = _pick(seq, SB_TQ)
    tk = _pick(tq, SB_TK)
    hp = _pick(heads, SB_HP)
    nq = seq // tq
    wd = hp * HEAD_DIM
    return pl.pallas_call(
        functools.partial(_sb_kernel, tq=tq, tk=tk, hp=hp),
        grid=(bsz, heads // hp, nq),
        in_specs=[pl.BlockSpec((tq, wd), lambda b, h, i: (b * nq + i, q_blk // hp + h)),
                  pl.BlockSpec((seq, wd), lambda b, h, i: (b, k_blk // hp + h)),
                  pl.BlockSpec((seq, wd), lambda b, h, i: (b, v_blk // hp + h))],
        out_specs=pl.BlockSpec((tq, wd), lambda b, h, i: (b * nq + i, h)),
        out_shape=jax.ShapeDtypeStruct((bsz * seq, heads * HEAD_DIM), BF16),
        compiler_params=_cparams(("parallel", "parallel", "arbitrary")),
        name="sb_attention",
    )(proj, proj, proj)


def _merge_kernel(of_ref, os_ref, wa_ref, wb_ref, gf_ref, gs_ref, o_ref):
    yf = _dot(of_ref[...], wa_ref[...])
    ys = _dot(os_ref[...], wb_ref[...])
    o_ref[...] = (gf_ref[...].astype(F32) * yf + gs_ref[...].astype(F32) * ys).astype(o_ref.dtype)


def _merge(o_fox, o_sb, wa, wb, proj, g_col0, d):
    t = o_fox.shape[0]
    tm = _pick(t, 512)
    tn = _pick(d, 1024)
    gf0 = g_col0 // tn
    gs0 = (g_col0 + d) // tn
    return pl.pallas_call(
        _merge_kernel,
        grid=(t // tm, d // tn),
        in_specs=[pl.BlockSpec((tm, o_fox.shape[1]), lambda i, j: (i, 0)),
                  pl.BlockSpec((tm, o_sb.shape[1]), lambda i, j: (i, 0)),
                  pl.BlockSpec((wa.shape[0], tn), lambda i, j: (0, j)),
                  pl.BlockSpec((wb.shape[0], tn), lambda i, j: (0, j)),
                  pl.BlockSpec((tm, tn), lambda i, j: (i, gf0 + j)),
                  pl.BlockSpec((tm, tn), lambda i, j: (i, gs0 + j))],
        out_specs=pl.BlockSpec((tm, tn), lambda i, j: (i, j)),
        out_shape=jax.ShapeDtypeStruct((t, d), BF16),
        compiler_params=_cparams(("parallel", "arbitrary")),
        name="branch_merge",
    )(o_fox, o_sb, wa, wb, proj, proj)


def _layer_norm(v, g, b):
    mu = jnp.mean(v, axis=-1, keepdims=True)
    c = v - mu
    var = jnp.mean(c * c, axis=-1, keepdims=True)
    return c * lax.rsqrt(var + LN_EPS) * g + b


def _pack_halves(a):
    half = a.shape[1] // 2
    lo = lax.bitcast_convert_type(a[:, :half].astype(BF16).astype(F32), jnp.uint32)
    hi = lax.bitcast_convert_type(a[:, half:].astype(BF16).astype(F32), jnp.uint32)
    return (hi & jnp.uint32(0xFFFF0000)) | (lo >> 16)


def _unpack_halves(p):
    lo = lax.bitcast_convert_type(p << 16, F32)
    hi = lax.bitcast_convert_type(p & jnp.uint32(0xFFFF0000), F32)
    return lo, hi


def _outln_kernel(m_ref, w_ref, x_ref, g1_ref, lg_ref, lb_ref, sc_ref, sh_ref,
                  x1_ref, h2_ref, y_ref, *, tn, alpha):
    j = pl.program_id(1)
    y_ref[:, pl.ds(pl.multiple_of(j * tn, tn), tn)] = _dot(m_ref[...], w_ref[...])

    @pl.when(j == pl.num_programs(1) - 1)
    def _():
        v = alpha * x_ref[...] + g1_ref[...] * y_ref[...]
        x1 = _layer_norm(v, lg_ref[...], lb_ref[...])
        x1_ref[...] = x1
        h2_ref[...] = _pack_halves(x1 * (1.0 + sc_ref[...]) + sh_ref[...])


def _out_ln(merged, w_out, x2, g1, ln_g, ln_b, sc2, sh2, seq, alpha):
    t, d = x2.shape
    tm = _pick(seq, 256)
    tn = _pick(d, 512)
    per_b = seq // tm
    row = lambda i, j: (i, 0)
    per_batch = lambda i, j: (i // per_b, 0, 0)
    const = lambda i, j: (0, 0)
    return pl.pallas_call(
        functools.partial(_outln_kernel, tn=tn, alpha=alpha),
        grid=(t // tm, d // tn),
        in_specs=[pl.BlockSpec((tm, d), row),
                  pl.BlockSpec((d, tn), lambda i, j: (0, j)),
                  pl.BlockSpec((tm, d), row),
                  pl.BlockSpec((None, 1, d), per_batch),
                  pl.BlockSpec((1, d), const),
                  pl.BlockSpec((1, d), const),
                  pl.BlockSpec((None, 1, d), per_batch),
                  pl.BlockSpec((None, 1, d), per_batch)],
        out_specs=[pl.BlockSpec((tm, d), row), pl.BlockSpec((tm, d // 2), row)],
        out_shape=[jax.ShapeDtypeStruct((t, d), F32), jax.ShapeDtypeStruct((t, d // 2), jnp.uint32)],
        scratch_shapes=[pltpu.VMEM((tm, d), F32)],
        compiler_params=_cparams(("parallel", "arbitrary")),
        name="out_proj_ln1",
    )(merged, w_out, x2, g1, ln_g, ln_b, sc2, sh2)


def _route_kernel(x1_ref, sc_ref, sh_ref, wr_ref, rb_ref, idx_ref, rank_ref, w_ref, cnt_ref, carry_ref,
                  *, n_exp):
    i = pl.program_id(0)

    @pl.when(i == 0)
    def _():
        carry_ref[...] = jnp.zeros_like(carry_ref)

    tb = x1_ref.shape[0]
    gsz = n_exp // N_GROUPS
    h = x1_ref[...] * (1.0 + sc_ref[...]) + sh_ref[...]
    logits = _dot_nt(wr_ref[...], h, precision=lax.Precision.HIGHEST)
    scores = 1.0 / (1.0 + jnp.exp(-logits))
    choice = scores + rb_ref[...][:, 0:1]
    neg = -jnp.inf

    g = choice.reshape(N_GROUPS, gsz, tb)
    io = lax.broadcasted_iota(I32, g.shape, 1)
    m1 = jnp.max(g, axis=1, keepdims=True)
    first = jnp.min(jnp.where(g == m1, io, gsz), axis=1, keepdims=True)
    m2 = jnp.max(jnp.where(io == first, neg, g), axis=1, keepdims=True)
    gs = (m1 + m2).reshape(N_GROUPS, tb)

    gio = lax.broadcasted_iota(I32, gs.shape, 0)
    gsel = jnp.zeros(gs.shape, F32)
    for _ in range(TOPK_GROUPS):
        m = jnp.max(gs, axis=0, keepdims=True)
        f = jnp.min(jnp.where(gs == m, gio, N_GROUPS), axis=0, keepdims=True)
        hit = gio == f
        gsel = jnp.where(hit, 1.0, gsel)
        gs = jnp.where(hit, neg, gs)
    emask = jnp.broadcast_to(gsel.reshape(N_GROUPS, 1, tb), (N_GROUPS, gsz, tb)).reshape(n_exp, tb)

    v = jnp.where(emask > 0.5, choice, neg)
    eio = lax.broadcasted_iota(I32, v.shape, 0)
    sel = jnp.zeros(v.shape, F32)
    idx_rows, w_rows = [], []
    for _ in range(TOP_K):
        m = jnp.max(v, axis=0, keepdims=True)
        f = jnp.min(jnp.where(v == m, eio, n_exp), axis=0, keepdims=True)
        hit = eio == f
        idx_rows.append(f)
        w_rows.append(jnp.sum(jnp.where(hit, scores, 0.0), axis=0, keepdims=True))
        sel = jnp.where(hit, 1.0, sel)
        v = jnp.where(hit, neg, v)
    wsum = w_rows[0]
    for r in range(1, TOP_K):
        wsum = wsum + w_rows[r]
    inv = 1.0 / (wsum + 1e-20)

    r_io = lax.broadcasted_iota(I32, (tb, tb), 0)
    c_io = lax.broadcasted_iota(I32, (tb, tb), 1)
    before = jnp.where(r_io < c_io, 1.0, 0.0).astype(BF16)
    excl = _dot(sel.astype(BF16), before) + carry_ref[...][:, 0:1]
    for r in range(TOP_K):
        hit = eio == idx_rows[r]
        rank = jnp.sum(jnp.where(hit, excl, 0.0), axis=0, keepdims=True)
        idx_ref[r:r + 1, :] = idx_rows[r]
        rank_ref[r:r + 1, :] = rank.astype(I32)
        w_ref[r:r + 1, :] = w_rows[r] * inv * ROUTED_SCALE
    total = carry_ref[...] + jnp.sum(sel, axis=1, keepdims=True)
    carry_ref[...] = total
    cnt_ref[...] = total.astype(I32)


def _route(x1, sc2, sh2, wr_t, rb, seq):
    t, d = x1.shape
    n_exp = wr_t.shape[0]
    tb = _pick(seq, 512)
    per_b = seq // tb
    blk = pl.BlockSpec((TOP_K, tb), lambda i: (0, i))
    return pl.pallas_call(
        functools.partial(_route_kernel, n_exp=n_exp),
        grid=(t // tb,),
        in_specs=[pl.BlockSpec((tb, d), lambda i: (i, 0)),
                  pl.BlockSpec((None, 1, d), lambda i: (i // per_b, 0, 0)),
                  pl.BlockSpec((None, 1, d), lambda i: (i // per_b, 0, 0)),
                  pl.BlockSpec((n_exp, d), lambda i: (0, 0)),
                  pl.BlockSpec((n_exp, LANES), lambda i: (0, 0))],
        out_specs=[blk, blk, blk, pl.BlockSpec((n_exp, LANES), lambda i: (0, 0))],
        out_shape=[jax.ShapeDtypeStruct((TOP_K, t), I32),
                   jax.ShapeDtypeStruct((TOP_K, t), I32),
                   jax.ShapeDtypeStruct((TOP_K, t), F32),
                   jax.ShapeDtypeStruct((n_exp, LANES), I32)],
        scratch_shapes=[pltpu.VMEM((n_exp, LANES), F32)],
        compiler_params=_cparams(("arbitrary",)),
        name="router_topk",
    )(x1, sc2, sh2, wr_t, rb)


def _dest_kernel(idx_ref, rank_ref, cnt_ref, dest_ref, be_ref, meta_ref, *, n_exp, tm, nbp):
    cnt = cnt_ref[...]
    nblk = jnp.right_shift(cnt + (tm - 1), tm.bit_length() - 1)
    r_io = lax.broadcasted_iota(I32, (n_exp, n_exp), 0)
    c_io = lax.broadcasted_iota(I32, (n_exp, n_exp), 1)
    below = jnp.where(c_io < r_io, 1.0, 0.0).astype(F32)
    start_blk = jnp.dot(below, nblk.astype(F32), preferred_element_type=F32,
                        precision=lax.Precision.HIGHEST).astype(I32)
    end_blk = start_blk + nblk
    start_row = start_blk * tm
    idx = idx_ref[...]
    dest = rank_ref[...]
    for e in range(n_exp):
        dest = dest + jnp.where(idx == e, start_row[e:e + 1, 0:1], 0)
    dest_ref[...] = dest
    b_io = lax.broadcasted_iota(I32, (n_exp, nbp), 1)
    done = jnp.sum(jnp.where(b_io >= end_blk[:, 0:1], 1.0, 0.0), axis=0, keepdims=True)
    be_ref[...] = jnp.minimum(done.astype(I32), n_exp - 1)
    meta_ref[...] = end_blk[n_exp - 1:n_exp, :]


def _dest(idx, rank, counts, n_exp, tm, n_blocks):
    assert tm & (tm - 1) == 0
    t = idx.shape[1]
    tb = _pick(t, 2048)
    nbp = -(-n_blocks // LANES) * LANES
    blk = pl.BlockSpec((TOP_K, tb), lambda i: (0, i))
    return pl.pallas_call(
        functools.partial(_dest_kernel, n_exp=n_exp, tm=tm, nbp=nbp),
        grid=(t // tb,),
        in_specs=[blk, blk, pl.BlockSpec((n_exp, LANES), lambda i: (0, 0))],
        out_specs=[blk, pl.BlockSpec((1, nbp), lambda i: (0, 0)),
                   pl.BlockSpec((1, LANES), lambda i: (0, 0))],
        out_shape=[jax.ShapeDtypeStruct((TOP_K, t), I32),
                   jax.ShapeDtypeStruct((1, nbp), I32),
                   jax.ShapeDtypeStruct((1, LANES), I32)],
        compiler_params=_cparams(("arbitrary",)),
        name="dispatch_rows",
    )(idx, rank, counts)


def _invert_kernel(dest_ref, tok_ref, *, chunk, n_tok, n_rows):
    c = pl.program_id(0)

    @pl.when(c == 0)
    def _():
        def zero(r, _):
            tok_ref[r] = 0
            return 0
        lax.fori_loop(0, n_rows, zero, 0, unroll=DMA_UNROLL)

    tok0 = (c * chunk) % n_tok

    def place(n, _):
        tok_ref[dest_ref[n]] = tok0 + n
        return 0
    lax.fori_loop(0, chunk, place, 0, unroll=DMA_UNROLL)


def _invert(dest_flat, n_tok, n_rows):
    n_assign = dest_flat.shape[0]
    chunk = _pick(n_tok, 8192)
    return pl.pallas_call(
        functools.partial(_invert_kernel, chunk=chunk, n_tok=n_tok, n_rows=n_rows),
        grid=(n_assign // chunk,),
        in_specs=[pl.BlockSpec((chunk,), lambda c: (c,), memory_space=pltpu.SMEM)],
        out_specs=pl.BlockSpec(memory_space=pltpu.SMEM),
        out_shape=jax.ShapeDtypeStruct((n_rows,), I32),
        compiler_params=_cparams(("arbitrary",)),
        name="dispatch_tokens",
    )(dest_flat)


def _swiglu_packed(xp, wg_ref, wu_ref, wd_ref):
    lo, hi = _unpack_halves(xp)
    x = jnp.concatenate([lo.astype(BF16), hi.astype(BF16)], axis=1)
    g = _dot(x, wg_ref[...])
    u = _dot(x, wu_ref[...])
    a = (g * (1.0 / (1.0 + jnp.exp(-g))) * u).astype(BF16)
    return _dot(a, wd_ref[...])


def _moe_kernel(tok_ref, be_ref, meta_ref, h_hbm, wg_ref, wu_ref, wd_ref, y_ref, xbuf, xp_ref, sem,
                *, tm):
    b = pl.program_id(0)
    used = meta_ref[0]

    def row_copy(blk, slot, r):
        tok = tok_ref[blk * tm + r]
        return pltpu.make_async_copy(h_hbm.at[pl.ds(tok, 1)], xbuf.at[slot, pl.ds(r, 1)],
                                     sem.at[slot])

    def gather(blk, slot):
        def issue(r, _):
            row_copy(blk, slot, r).start()
            return 0
        lax.fori_loop(0, tm, issue, 0, unroll=DMA_UNROLL)

    @pl.when(b == 0)
    def _():
        gather(0, 0)
        gather(1, 1)

    @pl.when(b <= used + 1)
    def _():
        slot = b % MOE_SLOTS
        pltpu.make_async_copy(h_hbm.at[pl.ds(0, tm)], xbuf.at[slot], sem.at[slot]).wait()

    @pl.when(b < used)
    def _():
        xp_ref[...] = xbuf[b % MOE_SLOTS]
        nxt = (b + 2) % MOE_SLOTS
        for r in range(tm):
            row_copy(b + 2, nxt, r).start()
        y_ref[...] = _pack_halves(_swiglu_packed(xp_ref[...], wg_ref, wu_ref, wd_ref))

    @pl.when(b >= used)
    def _():
        y_ref[...] = jnp.zeros_like(y_ref)


def _moe(tok_buf, block_e, meta, h2p, wg, wu, wd, tm):
    n_rows = tok_buf.shape[0]
    dp = h2p.shape[1]
    d = 2 * dp
    f = wg.shape[2]
    grid_spec = pltpu.PrefetchScalarGridSpec(
        num_scalar_prefetch=3,
        grid=(n_rows // tm,),
        in_specs=[pl.BlockSpec(memory_space=pl.ANY),
                  pl.BlockSpec((None, d, f), lambda b, tok, be, meta: (be[b], 0, 0)),
                  pl.BlockSpec((None, d, f), lambda b, tok, be, meta: (be[b], 0, 0)),
                  pl.BlockSpec((None, f, d), lambda b, tok, be, meta: (be[b], 0, 0))],
        out_specs=pl.BlockSpec((tm, dp), lambda b, tok, be, meta: (b, 0)),
        scratch_shapes=[pltpu.VMEM((MOE_SLOTS, tm, dp), jnp.uint32), pltpu.VMEM((tm, dp), jnp.uint32),
                        pltpu.SemaphoreType.DMA((MOE_SLOTS,))],
    )
    return pl.pallas_call(
        functools.partial(_moe_kernel, tm=tm),
        grid_spec=grid_spec,
        out_shape=jax.ShapeDtypeStruct((n_rows, dp), jnp.uint32),
        compiler_params=_cparams(("arbitrary",), disable_bounds_checks=True),
        name="routed_experts",
    )(tok_buf, block_e, meta, h2p, wg, wu, wd)


def _shared_kernel(h_ref, wg_ref, wu_ref, wd_ref, o_ref):
    o_ref[...] = _swiglu_packed(h_ref[...], wg_ref, wu_ref, wd_ref).astype(o_ref.dtype)


def _shared(h2p, wg, wu, wd):
    t, dp = h2p.shape
    d = 2 * dp
    f = wg.shape[1]
    tm = _pick(t, 512)
    return pl.pallas_call(
        _shared_kernel,
        grid=(t // tm,),
        in_specs=[pl.BlockSpec((tm, dp), lambda i: (i, 0)),
                  pl.BlockSpec((d, f), lambda i: (0, 0)),
                  pl.BlockSpec((d, f), lambda i: (0, 0)),
                  pl.BlockSpec((f, d), lambda i: (0, 0))],
        out_specs=pl.BlockSpec((tm, d), lambda i: (i, 0)),
        out_shape=jax.ShapeDtypeStruct((t, d), BF16),
        compiler_params=_cparams(("parallel",)),
        name="shared_expert",
    )(h2p, wg, wu, wd)


def _final_kernel(dest_ref, y_hbm, w_ref, x1_ref, s_ref, g2_ref, lg_ref, lb_ref, o_ref, ybuf, sem,
                  *, tb, n_tok, alpha):
    i = pl.program_id(0)
    nsteps = pl.num_programs(0)

    def gather(blk, slot):
        for r in range(TOP_K):
            base = r * n_tok + blk * tb

            def issue(n, _):
                row = dest_ref[base + n]
                pltpu.make_async_copy(y_hbm.at[pl.ds(row, 1)], ybuf.at[slot, pl.ds(r * tb + n, 1)],
                                      sem.at[slot]).start()
                return 0
            lax.fori_loop(0, tb, issue, 0, unroll=DMA_UNROLL)

    @pl.when(i == 0)
    def _():
        gather(0, 0)

    @pl.when(i + 1 < nsteps)
    def _():
        gather(i + 1, (i + 1) % 2)

    slot = i % 2
    pltpu.make_async_copy(y_hbm.at[pl.ds(0, TOP_K * tb)], ybuf.at[slot], sem.at[slot]).wait()

    w = w_ref[...]
    lo, hi = _unpack_halves(ybuf[slot, 0:tb])
    r_lo, r_hi = w[:, 0:1] * lo, w[:, 0:1] * hi
    for r in range(1, TOP_K):
        lo, hi = _unpack_halves(ybuf[slot, r * tb:(r + 1) * tb])
        r_lo, r_hi = r_lo + w[:, r:r + 1] * lo, r_hi + w[:, r:r + 1] * hi
    ffn = jnp.concatenate([r_lo, r_hi], axis=1) + s_ref[...].astype(F32)
    v = alpha * x1_ref[...] + g2_ref[...] * ffn
    o_ref[...] = _layer_norm(v, lg_ref[...], lb_ref[...])


def _final(dest, ys, w_tok, x1, shared, g2, ln_g, ln_b, seq, alpha):
    t, d = x1.shape
    tb = _pick(seq, COMBINE_TB)
    per_b = seq // tb
    grid_spec = pltpu.PrefetchScalarGridSpec(
        num_scalar_prefetch=1,
        grid=(t // tb,),
        in_specs=[pl.BlockSpec(memory_space=pl.ANY),
                  pl.BlockSpec((tb, TOP_K), lambda i, dest: (i, 0)),
                  pl.BlockSpec((tb, d), lambda i, dest: (i, 0)),
                  pl.BlockSpec((tb, d), lambda i, dest: (i, 0)),
                  pl.BlockSpec((None, 1, d), lambda i, dest: (i // per_b, 0, 0)),
                  pl.BlockSpec((1, d), lambda i, dest: (0, 0)),
                  pl.BlockSpec((1, d), lambda i, dest: (0, 0))],
        out_specs=pl.BlockSpec((tb, d), lambda i, dest: (i, 0)),
        scratch_shapes=[pltpu.VMEM((2, TOP_K * tb, d // 2), jnp.uint32),
                        pltpu.SemaphoreType.DMA((2,))],
    )
    return pl.pallas_call(
        functools.partial(_final_kernel, tb=tb, n_tok=t, alpha=alpha),
        grid_spec=grid_spec,
        out_shape=jax.ShapeDtypeStruct((t, d), F32),
        compiler_params=_cparams(("arbitrary",), disable_bounds_checks=True),
        name="combine_ln2",
    )(dest, ys, w_tok, x1, shared, g2, ln_g, ln_b)


def _layer(x2, c, bsz, seq, w_ada, b_ada, w_in, b_forget, b_gate, w_branch_fox, w_branch_sb, w_out,
           ln1_g, ln1_b, w_router, router_bias, w_exp_gate, w_exp_up, w_exp_down,
           w_sh_gate, w_sh_up, w_sh_down, ln2_g, ln2_b, alpha):
    t, d = x2.shape
    hf = b_forget.shape[0]
    fw = w_branch_fox.shape[0]
    sw = w_branch_sb.shape[0]
    hs = sw // HEAD_DIM
    n_exp = w_router.shape[1]

    mod = _ada(c, w_ada, b_ada)
    sh1, sc1, g1, sh2, sc2, g2 = [m.reshape(bsz, 1, d) for m in jnp.split(mod, 6, axis=-1)]

    f0 = 3 * fw
    qkv_cols = 3 * fw + 3 * sw
    qs = HEAD_DIM ** -0.5 * LOG2E
    w_packed = jnp.concatenate(
        [w_in[:, :fw] * qs, w_in[:, fw:f0], w_in[:, f0 + hf:f0 + hf + sw] * qs, w_in[:, f0 + hf + sw:]],
        axis=1).astype(BF16)
    bias_packed = jnp.concatenate([jnp.zeros((qkv_cols,), F32), b_gate]).reshape(1, -1)
    wf = jnp.zeros((d, LANES), F32).at[:, :hf].set(w_in[:, f0:f0 + hf]).astype(BF16)
    bf_pad = jnp.zeros((1, LANES), F32).at[0, :hf].set(b_forget)

    proj, fa = _inproj(x2, sc1, sh1, w_packed, bias_packed, wf, seq, qkv_cols)
    cum = _forget_cumsum(fa, bf_pad, seq)
    cum_rows = cum[:, :hf].reshape(bsz, seq, hf).transpose(0, 2, 1).reshape(bsz * hf, 1, seq)

    o_fox = _fox_attention(proj, cum_rows, bsz, seq, hf, 0, hf, 2 * hf)
    o_sb = _sb_attention(proj, bsz, seq, hs, 3 * hf, 3 * hf + hs, 3 * hf + 2 * hs)
    merged = _merge(o_fox, o_sb, w_branch_fox.astype(BF16), w_branch_sb.astype(BF16), proj,
                    qkv_cols, d)
    x1, h2 = _out_ln(merged, w_out.astype(BF16), x2, g1, ln1_g.reshape(1, d), ln1_b.reshape(1, d),
                     sc2, sh2, seq, alpha)

    rb = jnp.broadcast_to(router_bias.reshape(n_exp, 1), (n_exp, LANES))
    idx, rank, w_top, counts = _route(x1, sc2, sh2, w_router.T, rb, seq)
    tm = MOE_TM
    n_rows = t * TOP_K + (n_exp + 1) * tm
    dest2, be2, meta2 = _dest(idx, rank, counts, n_exp, tm, n_rows // tm)
    dest, block_e, meta = dest2.reshape(-1), be2[0, :n_rows // tm], meta2[0, :1]
    tok_buf = _invert(dest, t, n_rows)
    ys = _moe(tok_buf, block_e, meta, h2, w_exp_gate.astype(BF16), w_exp_up.astype(BF16),
              w_exp_down.astype(BF16), tm)
    shared = _shared(h2, w_sh_gate.astype(BF16), w_sh_up.astype(BF16), w_sh_down.astype(BF16))
    return _final(dest, ys, w_top.T, x1, shared, g2, ln2_g.reshape(1, d), ln2_b.reshape(1, d),
                  seq, alpha)


def kernel(x, c, w_ada, b_ada, w_in, b_forget, b_gate, w_branch_fox, w_branch_sb, w_out, ln1_g, ln1_b, w_router, router_bias, w_exp_gate, w_exp_up, w_exp_down, w_sh_gate, w_sh_up, w_sh_down, ln2_g, ln2_b):
    bsz, seq, d = x.shape
    depth = w_ada.shape[0]
    alpha = (2 * depth) ** 0.25
    x2 = x.reshape(bsz * seq, d)
    for l in range(depth):
        x2 = _layer(x2, c, bsz, seq, w_ada[l], b_ada[l], w_in[l], b_forget[l], b_gate[l],
                    w_branch_fox[l], w_branch_sb[l], w_out[l], ln1_g[l], ln1_b[l], w_router[l],
                    router_bias[l], w_exp_gate[l], w_exp_up[l], w_exp_down[l], w_sh_gate[l],
                    w_sh_up[l], w_sh_down[l], ln2_g[l], ln2_b[l], alpha)
    return x2.reshape(bsz, seq, d)
```

```python
import functools

import jax
import jax.numpy as jnp
from jax import lax
from jax.experimental import pallas as pl
from jax.experimental.pallas import tpu as pltpu

F32 = jnp.float32
BF16 = jnp.bfloat16
I32 = jnp.int32

HEAD_DIM = 128
LANES = 128
N_GROUPS = 8
TOPK_GROUPS = 4
TOP_K = 8
ROUTED_SCALE = 2.5
LN_EPS = 1e-5
VMEM_LIMIT = 56 * 1024 * 1024
NEG_BIG = -1e30
LOG2E = 1.4426950408889634
FOX_TQ, FOX_TK, FOX_HP = 512, 512, 2
SB_TQ, SB_TK, SB_HP = 512, 256, 4
MOE_TM = 256
MOE_SLOTS = 3
COMBINE_TB = 128
DMA_UNROLL = 8


def _cparams(sem, **kw):
    return pltpu.CompilerParams(dimension_semantics=sem, vmem_limit_bytes=VMEM_LIMIT, **kw)


def _pick(n, pref):
    t = min(n, pref)
    while n % t:
        t //= 2
    return t


def _dot(a, b):
    return jnp.dot(a, b, preferred_element_type=F32)


def _dot_nt(a, b, precision=None):
    return lax.dot_general(a, b, (((1,), (1,)), ((), ())), preferred_element_type=F32,
                           precision=precision)


def _ada_kernel(c_ref, w_ref, b_ref, o_ref):
    c = c_ref[...]
    s = c * (1.0 / (1.0 + jnp.exp(-c)))
    o_ref[...] = jnp.dot(s, w_ref[...], preferred_element_type=F32,
                         precision=lax.Precision.HIGHEST) + b_ref[...]


def _ada(c, w_ada, b_ada):
    bsz, d = c.shape
    n = w_ada.shape[1]
    rows = 8
    c_pad = jnp.zeros((rows, d), F32).at[:bsz].set(c)
    tn = _pick(n, 512)
    out = pl.pallas_call(
        _ada_kernel,
        grid=(n // tn,),
        in_specs=[pl.BlockSpec((rows, d), lambda j: (0, 0)),
                  pl.BlockSpec((d, tn), lambda j: (0, j)),
                  pl.BlockSpec((1, tn), lambda j: (0, j))],
        out_specs=pl.BlockSpec((rows, tn), lambda j: (0, j)),
        out_shape=jax.ShapeDtypeStruct((rows, n), F32),
        compiler_params=_cparams(("parallel",)),
        name="ada_mod",
    )(c_pad, w_ada, b_ada.reshape(1, n))
    return out[:bsz]


def _inproj_kernel(x_ref, sc_ref, sh_ref, w_ref, b_ref, wf_ref, o_ref, f_ref, h_ref, *, n_plain):
    j = pl.program_id(1)

    @pl.when(j == 0)
    def _():
        h = (x_ref[...] * (1.0 + sc_ref[...]) + sh_ref[...]).astype(BF16)
        h_ref[...] = h
        f_ref[...] = _dot_nt(h, wf_ref[...])

    acc = _dot_nt(h_ref[...], w_ref[...])

    @pl.when(j < n_plain)
    def _():
        o_ref[...] = acc.astype(o_ref.dtype)

    @pl.when(j >= n_plain)
    def _():
        z = acc + b_ref[...]
        o_ref[...] = (1.0 / (1.0 + jnp.exp(-z))).astype(o_ref.dtype)


def _inproj(x2, sc, sh, w_packed, bias_packed, wf, seq, n_plain_cols):
    t, d = x2.shape
    n = w_packed.shape[0]
    tm = _pick(seq, 512)
    tn = _pick(n_plain_cols, 1024)
    while (n - n_plain_cols) % tn:
        tn //= 2
    per_b = seq // tm
    return pl.pallas_call(
        functools.partial(_inproj_kernel, n_plain=n_plain_cols // tn),
        grid=(t // tm, n // tn),
        in_specs=[pl.BlockSpec((tm, d), lambda i, j: (i, 0)),
                  pl.BlockSpec((None, 1, d), lambda i, j: (i // per_b, 0, 0)),
                  pl.BlockSpec((None, 1, d), lambda i, j: (i // per_b, 0, 0)),
                  pl.BlockSpec((tn, d), lambda i, j: (j, 0)),
                  pl.BlockSpec((1, tn), lambda i, j: (0, j)),
                  pl.BlockSpec((LANES, d), lambda i, j: (0, 0))],
        out_specs=[pl.BlockSpec((tm, tn), lambda i, j: (i, j)),
                   pl.BlockSpec((tm, LANES), lambda i, j: (i, 0))],
        out_shape=[jax.ShapeDtypeStruct((t, n), BF16),
                   jax.ShapeDtypeStruct((t, LANES), F32)],
        scratch_shapes=[pltpu.VMEM((tm, d), BF16)],
        compiler_params=_cparams(("parallel", "arbitrary")),
        name="in_proj",
    )(x2, sc, sh, w_packed, bias_packed, wf)


def _cum_kernel(f_ref, b_ref, o_ref, carry_ref, *, per_b):
    i = pl.program_id(0)

    @pl.when(i % per_b == 0)
    def _():
        carry_ref[...] = jnp.zeros_like(carry_ref)

    z = f_ref[...] + b_ref[...]
    lf = jnp.minimum(z, 0.0) - jnp.log(1.0 + jnp.exp(-jnp.abs(z)))
    tb = lf.shape[0]
    row = lax.broadcasted_iota(I32, (tb, tb), 0)
    col = lax.broadcasted_iota(I32, (tb, tb), 1)
    tri = jnp.where(col <= row, 1.0, 0.0).astype(F32)
    cum = jnp.dot(tri, lf, preferred_element_type=F32,
                  precision=lax.Precision.HIGHEST) + carry_ref[...]
    o_ref[...] = cum
    carry_ref[...] = cum[tb - 1:tb, :]


def _forget_cumsum(fa, bf_pad, seq):
    t = fa.shape[0]
    tb = _pick(seq, 512)
    return pl.pallas_call(
        functools.partial(_cum_kernel, per_b=seq // tb),
        grid=(t // tb,),
        in_specs=[pl.BlockSpec((tb, LANES), lambda i: (i, 0)),
                  pl.BlockSpec((1, LANES), lambda i: (0, 0))],
        out_specs=pl.BlockSpec((tb, LANES), lambda i: (i, 0)),
        out_shape=jax.ShapeDtypeStruct((t, LANES), F32),
        scratch_shapes=[pltpu.VMEM((1, LANES), F32)],
        compiler_params=_cparams(("arbitrary",)),
        name="forget_cumsum",
    )(fa, bf_pad)


def _hs(h):
    return slice(h * HEAD_DIM, (h + 1) * HEAD_DIM)


def _fox_kernel(q_ref, k_ref, v_ref, ck_ref, o_ref, *, tq, tk, hp):
    i = pl.program_id(2)
    q0 = pl.multiple_of(i * tq, tq)
    heads = range(hp)
    n_diag = tq // tk
    qs = [q_ref[:, _hs(h)] for h in heads]
    c0s = [ck_ref[h, :, pl.ds(q0, tk)][:, 0:1] for h in heads]

    def scores(h, k0):
        s = _dot_nt(qs[h], k_ref[pl.ds(k0, tk), _hs(h)])
        return s + (c0s[h] - ck_ref[h, :, pl.ds(k0, tk)]) * LOG2E

    def update(h, s, k0, m, l, acc):
        m_new = jnp.maximum(m, jnp.max(s, axis=1, keepdims=True))
        p = jnp.exp2(s - m_new)
        alpha = jnp.exp2(m - m_new)
        l = alpha * l + jnp.sum(p, axis=1, keepdims=True)
        acc = alpha * acc + _dot(p.astype(BF16), v_ref[pl.ds(k0, tk), _hs(h)])
        return m_new, l, acc

    def body(j, carry):
        carry = list(carry)
        for u in range(n_diag):
            k0 = pl.multiple_of((j * n_diag + u) * tk, tk)
            for h in heads:
                carry[h] = update(h, scores(h, k0), k0, *carry[h])
        return tuple(carry)

    init = tuple((jnp.full((tq, 1), NEG_BIG, F32), jnp.zeros((tq, 1), F32),
                  jnp.zeros((tq, HEAD_DIM), F32)) for _ in heads)
    carry = list(lax.fori_loop(0, i, body, init))
    row = lax.broadcasted_iota(I32, (tq, tk), 0)
    col = lax.broadcasted_iota(I32, (tq, tk), 1)
    for dblk in range(n_diag):
        off = dblk * tk
        k0 = pl.multiple_of(q0 + off, tk)
        for h in heads:
            s = jnp.where(col + off <= row, scores(h, k0), NEG_BIG)
            carry[h] = update(h, s, k0, *carry[h])
    for h in heads:
        _, l, acc = carry[h]
        o_ref[:, _hs(h)] = (acc / l).astype(o_ref.dtype)


def _fox_attention(proj, cum_rows, bsz, seq, heads, q_blk, k_blk, v_blk):
    tq = _pick(seq, FOX_TQ)
    tk = _pick(tq, FOX_TK)
    hp = _pick(heads, FOX_HP)
    nq = seq // tq
    wd = hp * HEAD_DIM
    return pl.pallas_call(
        functools.partial(_fox_kernel, tq=tq, tk=tk, hp=hp),
        grid=(bsz, heads // hp, nq),
        in_specs=[pl.BlockSpec((tq, wd), lambda b, h, i: (b * nq + i, q_blk // hp + h)),
                  pl.BlockSpec((seq, wd), lambda b, h, i: (b, k_blk // hp + h)),
                  pl.BlockSpec((seq, wd), lambda b, h, i: (b, v_blk // hp + h)),
                  pl.BlockSpec((hp, 1, seq), lambda b, h, i: (b * (heads // hp) + h, 0, 0))],
        out_specs=pl.BlockSpec((tq, wd), lambda b, h, i: (b * nq + i, h)),
        out_shape=jax.ShapeDtypeStruct((bsz * seq, heads * HEAD_DIM), BF16),
        compiler_params=_cparams(("parallel", "parallel", "arbitrary")),
        name="fox_attention",
    )(proj, proj, proj, cum_rows)


def _sb_kernel(q_ref, k_ref, v_ref, o_ref, *, tq, tk, hp):
    i = pl.program_id(2)
    q0 = pl.multiple_of(i * tq, tq)
    heads = range(hp)
    n_diag = tq // tk
    r_io = lax.broadcasted_iota(I32, (tk, tk), 0)
    c_io = lax.broadcasted_iota(I32, (tk, tk), 1)
    not_before = jnp.where(r_io >= c_io, 1.0, 0.0).astype(BF16)
    row = lax.broadcasted_iota(I32, (tq, tk), 0)
    col = lax.broadcasted_iota(I32, (tq, tk), 1)
    qs = [q_ref[:, _hs(h)] for h in heads]

    def tile(h, k0):
        z = _dot_nt(qs[h], k_ref[pl.ds(k0, tk), _hs(h)])
        sp = jnp.maximum(z, 0.0) + jnp.log2(1.0 + jnp.exp2(-jnp.abs(z)))
        return z, sp

    runs = [jnp.zeros((tq, 1), F32) for _ in heads]
    accs = [jnp.zeros((tq, HEAD_DIM), F32) for _ in heads]
    for dblk in range(n_diag):
        off = (n_diag - 1 - dblk) * tk
        valid = col + off < row
        k0 = pl.multiple_of(q0 + off, tk)
        for h in heads:
            z, sp = tile(h, k0)
            sp = jnp.where(valid, sp, 0.0)
            suffix = _dot(sp.astype(BF16), not_before)
            w = jnp.where(valid, jnp.exp2(z - suffix - runs[h]), 0.0)
            accs[h] = accs[h] + _dot(w.astype(BF16), v_ref[pl.ds(k0, tk), _hs(h)])
            runs[h] = runs[h] + suffix[:, 0:1]

    def body(n, carry):
        runs, accs = list(carry[0]), list(carry[1])
        for u in range(n_diag):
            k0 = pl.multiple_of(q0 - (n * n_diag + u + 1) * tk, tk)
            for h in heads:
                z, sp = tile(h, k0)
                suffix = _dot(sp.astype(BF16), not_before)
                w = jnp.exp2(z - suffix - runs[h])
                accs[h] = accs[h] + _dot(w.astype(BF16), v_ref[pl.ds(k0, tk), _hs(h)])
                runs[h] = runs[h] + suffix[:, 0:1]
        return tuple(runs), tuple(accs)

    _, accs = lax.fori_loop(0, i, body, (tuple(runs), tuple(accs)))
    for h in heads:
        o_ref[:, _hs(h)] = accs[h].astype(o_ref.dtype)


def _sb_attention(proj, bsz, seq, heads, q_blk, k_blk, v_blk):
    tq = _pick(seq, SB_TQ)
    tk = _pick(tq, SB_TK)
    hp = _pick(heads, SB_HP)
    nq = seq // tq
    wd = hp * HEAD_DIM
    return pl.pallas_call(
        functools.partial(_sb_kernel, tq=tq, tk=tk, hp=hp),
        grid=(bsz, heads // hp, nq),
        in_specs=[pl.BlockSpec((tq, wd), lambda b, h, i: (b * nq + i, q_blk // hp + h)),
                  pl.BlockSpec((seq, wd), lambda b, h, i: (b, k_blk // hp + h)),
                  pl.BlockSpec((seq, wd), lambda b, h, i: (b, v_blk // hp + h))],
        out_specs=pl.BlockSpec((tq, wd), lambda b, h, i: (b * nq + i, h)),
        out_shape=jax.ShapeDtypeStruct((bsz * seq, heads * HEAD_DIM), BF16),
        compiler_params=_cparams(("parallel", "parallel", "arbitrary")),
        name="sb_attention",
    )(proj, proj, proj)


def _merge_kernel(of_ref, os_ref, wa_ref, wb_ref, gf_ref, gs_ref, o_ref):
    yf = _dot(of_ref[...], wa_ref[...])
    ys = _dot(os_ref[...], wb_ref[...])
    o_ref[...] = (gf_ref[...].astype(F32) * yf + gs_ref[...].astype(F32) * ys).astype(o_ref.dtype)


def _merge(o_fox, o_sb, wa, wb, proj, g_col0, d):
    t = o_fox.shape[0]
    tm = _pick(t, 512)
    tn = _pick(d, 1024)
    gf0 = g_col0 // tn
    gs0 = (g_col0 + d) // tn
    return pl.pallas_call(
        _merge_kernel,
        grid=(t // tm, d // tn),
        in_specs=[pl.BlockSpec((tm, o_fox.shape[1]), lambda i, j: (i, 0)),
                  pl.BlockSpec((tm, o_sb.shape[1]), lambda i, j: (i, 0)),
                  pl.BlockSpec((wa.shape[0], tn), lambda i, j: (0, j)),
                  pl.BlockSpec((wb.shape[0], tn), lambda i, j: (0, j)),
                  pl.BlockSpec((tm, tn), lambda i, j: (i, gf0 + j)),
                  pl.BlockSpec((tm, tn), lambda i, j: (i, gs0 + j))],
        out_specs=pl.BlockSpec((tm, tn), lambda i, j: (i, j)),
        out_shape=jax.ShapeDtypeStruct((t, d), BF16),
        compiler_params=_cparams(("parallel", "arbitrary")),
        name="branch_merge",
    )(o_fox, o_sb, wa, wb, proj, proj)


def _layer_norm(v, g, b):
    mu = jnp.mean(v, axis=-1, keepdims=True)
    c = v - mu
    var = jnp.mean(c * c, axis=-1, keepdims=True)
    return c * lax.rsqrt(var + LN_EPS) * g + b


def _pack_halves(a):
    half = a.shape[1] // 2
    lo = lax.bitcast_convert_type(a[:, :half].astype(BF16).astype(F32), jnp.uint32)
    hi = lax.bitcast_convert_type(a[:, half:].astype(BF16).astype(F32), jnp.uint32)
    return (hi & jnp.uint32(0xFFFF0000)) | (lo >> 16)


def _unpack_halves(p):
    lo = lax.bitcast_convert_type(p << 16, F32)
    hi = lax.bitcast_convert_type(p & jnp.uint32(0xFFFF0000), F32)
    return lo, hi


def _matmul_kernel(a_ref, w_ref, o_ref):
    o_ref[...] = _dot(a_ref[...], w_ref[...]).astype(o_ref.dtype)


def _out_proj(merged, w_out):
    t, d = merged.shape
    n = w_out.shape[1]
    tm = _pick(t, 1024)
    tn = _pick(n, 1024)
    return pl.pallas_call(
        _matmul_kernel,
        grid=(t // tm, n // tn),
        in_specs=[pl.BlockSpec((tm, d), lambda i, j: (i, 0)),
                  pl.BlockSpec((d, tn), lambda i, j: (0, j))],
        out_specs=pl.BlockSpec((tm, tn), lambda i, j: (i, j)),
        out_shape=jax.ShapeDtypeStruct((t, n), BF16),
        compiler_params=_cparams(("parallel", "arbitrary")),
        name="out_proj",
    )(merged, w_out)


def _ln1_kernel(y_ref, x_ref, g1_ref, lg_ref, lb_ref, sc_ref, sh_ref, x1_ref, h2_ref, *, alpha):
    v = alpha * x_ref[...] + g1_ref[...] * y_ref[...].astype(F32)
    x1 = _layer_norm(v, lg_ref[...], lb_ref[...])
    x1_ref[...] = x1
    h2_ref[...] = _pack_halves(x1 * (1.0 + sc_ref[...]) + sh_ref[...])


def _ln1(mix, x2, g1, ln_g, ln_b, sc2, sh2, seq, alpha):
    t, d = x2.shape
    tm = _pick(seq, 256)
    per_b = seq // tm
    row = lambda i: (i, 0)
    per_batch = lambda i: (i // per_b, 0, 0)
    const = lambda i: (0, 0)
    return pl.pallas_call(
        functools.partial(_ln1_kernel, alpha=alpha),
        grid=(t // tm,),
        in_specs=[pl.BlockSpec((tm, d), row),
                  pl.BlockSpec((tm, d), row),
                  pl.BlockSpec((None, 1, d), per_batch),
                  pl.BlockSpec((1, d), const),
                  pl.BlockSpec((1, d), const),
                  pl.BlockSpec((None, 1, d), per_batch),
                  pl.BlockSpec((None, 1, d), per_batch)],
        out_specs=[pl.BlockSpec((tm, d), row), pl.BlockSpec((tm, d // 2), row)],
        out_shape=[jax.ShapeDtypeStruct((t, d), F32), jax.ShapeDtypeStruct((t, d // 2), jnp.uint32)],
        compiler_params=_cparams(("parallel",)),
        name="residual_ln1",
    )(mix, x2, g1, ln_g, ln_b, sc2, sh2)


def _route_kernel(x1_ref, sc_ref, sh_ref, wr_ref, rb_ref, idx_ref, rank_ref, w_ref, cnt_ref, carry_ref,
                  *, n_exp):
    i = pl.program_id(0)

    @pl.when(i == 0)
    def _():
        carry_ref[...] = jnp.zeros_like(carry_ref)

    tb = x1_ref.shape[0]
    gsz = n_exp // N_GROUPS
    h = x1_ref[...] * (1.0 + sc_ref[...]) + sh_ref[...]
    logits = _dot_nt(wr_ref[...], h, precision=lax.Precision.HIGHEST)
    scores = 1.0 / (1.0 + jnp.exp(-logits))
    choice = scores + rb_ref[...][:, 0:1]
    neg = -jnp.inf

    g = choice.reshape(N_GROUPS, gsz, tb)
    io = lax.broadcasted_iota(I32, g.shape, 1)
    m1 = jnp.max(g, axis=1, keepdims=True)
    first = jnp.min(jnp.where(g == m1, io, gsz), axis=1, keepdims=True)
    m2 = jnp.max(jnp.where(io == first, neg, g), axis=1, keepdims=True)
    gs = (m1 + m2).reshape(N_GROUPS, tb)

    gio = lax.broadcasted_iota(I32, gs.shape, 0)
    gsel = jnp.zeros(gs.shape, F32)
    for _ in range(TOPK_GROUPS):
        m = jnp.max(gs, axis=0, keepdims=True)
        f = jnp.min(jnp.where(gs == m, gio, N_GROUPS), axis=0, keepdims=True)
        hit = gio == f
        gsel = jnp.where(hit, 1.0, gsel)
        gs = jnp.where(hit, neg, gs)
    emask = jnp.broadcast_to(gsel.reshape(N_GROUPS, 1, tb), (N_GROUPS, gsz, tb)).reshape(n_exp, tb)

    v = jnp.where(emask > 0.5, choice, neg)
    eio = lax.broadcasted_iota(I32, v.shape, 0)
    sel = jnp.zeros(v.shape, F32)
    idx_rows, w_rows = [], []
    for _ in range(TOP_K):
        m = jnp.max(v, axis=0, keepdims=True)
        f = jnp.min(jnp.where(v == m, eio, n_exp), axis=0, keepdims=True)
        hit = eio == f
        idx_rows.append(f)
        w_rows.append(jnp.sum(jnp.where(hit, scores, 0.0), axis=0, keepdims=True))
        sel = jnp.where(hit, 1.0, sel)
        v = jnp.where(hit, neg, v)
    wsum = w_rows[0]
    for r in range(1, TOP_K):
        wsum = wsum + w_rows[r]
    inv = 1.0 / (wsum + 1e-20)

    r_io = lax.broadcasted_iota(I32, (tb, tb), 0)
    c_io = lax.broadcasted_iota(I32, (tb, tb), 1)
    before = jnp.where(r_io < c_io, 1.0, 0.0).astype(BF16)
    excl = _dot(sel.astype(BF16), before) + carry_ref[...][:, 0:1]
    for r in range(TOP_K):
        hit = eio == idx_rows[r]
        rank = jnp.sum(jnp.where(hit, excl, 0.0), axis=0, keepdims=True)
        idx_ref[r:r + 1, :] = idx_rows[r]
        rank_ref[r:r + 1, :] = rank.astype(I32)
        w_ref[r:r + 1, :] = w_rows[r] * inv * ROUTED_SCALE
    total = carry_ref[...] + jnp.sum(sel, axis=1, keepdims=True)
    carry_ref[...] = total
    cnt_ref[...] = total.astype(I32)


def _route(x1, sc2, sh2, wr_t, rb, seq):
    t, d = x1.shape
    n_exp = wr_t.shape[0]
    tb = _pick(seq, 512)
    per_b = seq // tb
    blk = pl.BlockSpec((TOP_K, tb), lambda i: (0, i))
    return pl.pallas_call(
        functools.partial(_route_kernel, n_exp=n_exp),
        grid=(t // tb,),
        in_specs=[pl.BlockSpec((tb, d), lambda i: (i, 0)),
                  pl.BlockSpec((None, 1, d), lambda i: (i // per_b, 0, 0)),
                  pl.BlockSpec((None, 1, d), lambda i: (i // per_b, 0, 0)),
                  pl.BlockSpec((n_exp, d), lambda i: (0, 0)),
                  pl.BlockSpec((n_exp, LANES), lambda i: (0, 0))],
        out_specs=[blk, blk, blk, pl.BlockSpec((n_exp, LANES), lambda i: (0, 0))],
        out_shape=[jax.ShapeDtypeStruct((TOP_K, t), I32),
                   jax.ShapeDtypeStruct((TOP_K, t), I32),
                   jax.ShapeDtypeStruct((TOP_K, t), F32),
                   jax.ShapeDtypeStruct((n_exp, LANES), I32)],
        scratch_shapes=[pltpu.VMEM((n_exp, LANES), F32)],
        compiler_params=_cparams(("arbitrary",)),
        name="router_topk",
    )(x1, sc2, sh2, wr_t, rb)


def _dest_kernel(idx_ref, rank_ref, cnt_ref, dest_ref, be_ref, meta_ref, *, n_exp, tm, nbp):
    cnt = cnt_ref[...]
    nblk = jnp.right_shift(cnt + (tm - 1), tm.bit_length() - 1)
    r_io = lax.broadcasted_iota(I32, (n_exp, n_exp), 0)
    c_io = lax.broadcasted_iota(I32, (n_exp, n_exp), 1)
    below = jnp.where(c_io < r_io, 1.0, 0.0).astype(F32)
    start_blk = jnp.dot(below, nblk.astype(F32), preferred_element_type=F32,
                        precision=lax.Precision.HIGHEST).astype(I32)
    end_blk = start_blk + nblk
    start_row = start_blk * tm
    idx = idx_ref[...]
    dest = rank_ref[...]
    for e in range(n_exp):
        dest = dest + jnp.where(idx == e, start_row[e:e + 1, 0:1], 0)
    dest_ref[...] = dest
    b_io = lax.broadcasted_iota(I32, (n_exp, nbp), 1)
    done = jnp.sum(jnp.where(b_io >= end_blk[:, 0:1], 1.0, 0.0), axis=0, keepdims=True)
    be_ref[...] = jnp.minimum(done.astype(I32), n_exp - 1)
    meta_ref[...] = end_blk[n_exp - 1:n_exp, :]


def _dest(idx, rank, counts, n_exp, tm, n_blocks):
    assert tm & (tm - 1) == 0
    t = idx.shape[1]
    tb = _pick(t, 2048)
    nbp = -(-n_blocks // LANES) * LANES
    blk = pl.BlockSpec((TOP_K, tb), lambda i: (0, i))
    return pl.pallas_call(
        functools.partial(_dest_kernel, n_exp=n_exp, tm=tm, nbp=nbp),
        grid=(t // tb,),
        in_specs=[blk, blk, pl.BlockSpec((n_exp, LANES), lambda i: (0, 0))],
        out_specs=[blk, pl.BlockSpec((1, nbp), lambda i: (0, 0)),
                   pl.BlockSpec((1, LANES), lambda i: (0, 0))],
        out_shape=[jax.ShapeDtypeStruct((TOP_K, t), I32),
                   jax.ShapeDtypeStruct((1, nbp), I32),
                   jax.ShapeDtypeStruct((1, LANES), I32)],
        compiler_params=_cparams(("arbitrary",)),
        name="dispatch_rows",
    )(idx, rank, counts)


def _invert_kernel(dest_ref, tok_ref, *, chunk, n_tok, n_rows):
    c = pl.program_id(0)

    @pl.when(c == 0)
    def _():
        def zero(r, _):
            tok_ref[r] = 0
            return 0
        lax.fori_loop(0, n_rows, zero, 0, unroll=DMA_UNROLL)

    tok0 = (c * chunk) % n_tok

    def place(n, _):
        tok_ref[dest_ref[n]] = tok0 + n
        return 0
    lax.fori_loop(0, chunk, place, 0, unroll=DMA_UNROLL)


def _invert(dest_flat, n_tok, n_rows):
    n_assign = dest_flat.shape[0]
    chunk = _pick(n_tok, 8192)
    return pl.pallas_call(
        functools.partial(_invert_kernel, chunk=chunk, n_tok=n_tok, n_rows=n_rows),
        grid=(n_assign // chunk,),
        in_specs=[pl.BlockSpec((chunk,), lambda c: (c,), memory_space=pltpu.SMEM)],
        out_specs=pl.BlockSpec(memory_space=pltpu.SMEM),
        out_shape=jax.ShapeDtypeStruct((n_rows,), I32),
        compiler_params=_cparams(("arbitrary",)),
        name="dispatch_tokens",
    )(dest_flat)


def _swiglu_packed(xp, wg_ref, wu_ref, wd_ref):
    lo, hi = _unpack_halves(xp)
    x = jnp.concatenate([lo.astype(BF16), hi.astype(BF16)], axis=1)
    g = _dot(x, wg_ref[...])
    u = _dot(x, wu_ref[...])
    a = (g * (1.0 / (1.0 + jnp.exp(-g))) * u).astype(BF16)
    return _dot(a, wd_ref[...])


def _moe_kernel(tok_ref, be_ref, meta_ref, h_hbm, wg_ref, wu_ref, wd_ref, y_ref, xbuf, xp_ref, sem,
                *, tm):
    b = pl.program_id(0)
    used = meta_ref[0]

    def row_copy(blk, slot, r):
        tok = tok_ref[blk * tm + r]
        return pltpu.make_async_copy(h_hbm.at[pl.ds(tok, 1)], xbuf.at[slot, pl.ds(r, 1)],
                                     sem.at[slot])

    def gather(blk, slot):
        def issue(r, _):
            row_copy(blk, slot, r).start()
            return 0
        lax.fori_loop(0, tm, issue, 0, unroll=DMA_UNROLL)

    @pl.when(b == 0)
    def _():
        gather(0, 0)
        gather(1, 1)

    @pl.when(b <= used + 1)
    def _():
        slot = b % MOE_SLOTS
        pltpu.make_async_copy(h_hbm.at[pl.ds(0, tm)], xbuf.at[slot], sem.at[slot]).wait()

    @pl.when(b < used)
    def _():
        xp_ref[...] = xbuf[b % MOE_SLOTS]
        nxt = (b + 2) % MOE_SLOTS
        for r in range(tm):
            row_copy(b + 2, nxt, r).start()
        y_ref[...] = _pack_halves(_swiglu_packed(xp_ref[...], wg_ref, wu_ref, wd_ref))

    @pl.when(b >= used)
    def _():
        y_ref[...] = jnp.zeros_like(y_ref)


def _moe(tok_buf, block_e, meta, h2p, wg, wu, wd, tm):
    n_rows = tok_buf.shape[0]
    dp = h2p.shape[1]
    d = 2 * dp
    f = wg.shape[2]
    grid_spec = pltpu.PrefetchScalarGridSpec(
        num_scalar_prefetch=3,
        grid=(n_rows // tm,),
        in_specs=[pl.BlockSpec(memory_space=pl.ANY),
                  pl.BlockSpec((None, d, f), lambda b, tok, be, meta: (be[b], 0, 0)),
                  pl.BlockSpec((None, d, f), lambda b, tok, be, meta: (be[b], 0, 0)),
                  pl.BlockSpec((None, f, d), lambda b, tok, be, meta: (be[b], 0, 0))],
        out_specs=pl.BlockSpec((tm, dp), lambda b, tok, be, meta: (b, 0)),
        scratch_shapes=[pltpu.VMEM((MOE_SLOTS, tm, dp), jnp.uint32), pltpu.VMEM((tm, dp), jnp.uint32),
                        pltpu.SemaphoreType.DMA((MOE_SLOTS,))],
    )
    return pl.pallas_call(
        functools.partial(_moe_kernel, tm=tm),
        grid_spec=grid_spec,
        out_shape=jax.ShapeDtypeStruct((n_rows, dp), jnp.uint32),
        compiler_params=_cparams(("arbitrary",), disable_bounds_checks=True),
        name="routed_experts",
    )(tok_buf, block_e, meta, h2p, wg, wu, wd)


def _shared_kernel(h_ref, wg_ref, wu_ref, wd_ref, o_ref):
    o_ref[...] = _swiglu_packed(h_ref[...], wg_ref, wu_ref, wd_ref).astype(o_ref.dtype)


def _shared(h2p, wg, wu, wd):
    t, dp = h2p.shape
    d = 2 * dp
    f = wg.shape[1]
    tm = _pick(t, 512)
    return pl.pallas_call(
        _shared_kernel,
        grid=(t // tm,),
        in_specs=[pl.BlockSpec((tm, dp), lambda i: (i, 0)),
                  pl.BlockSpec((d, f), lambda i: (0, 0)),
                  pl.BlockSpec((d, f), lambda i: (0, 0)),
                  pl.BlockSpec((f, d), lambda i: (0, 0))],
        out_specs=pl.BlockSpec((tm, d), lambda i: (i, 0)),
        out_shape=jax.ShapeDtypeStruct((t, d), BF16),
        compiler_params=_cparams(("parallel",)),
        name="shared_expert",
    )(h2p, wg, wu, wd)


def _final_kernel(dest_ref, y_hbm, w_ref, x1_ref, s_ref, g2_ref, lg_ref, lb_ref, o_ref, ybuf, sem,
                  *, tb, n_tok, alpha, last_slot):
    i = pl.program_id(0)
    last = pl.num_programs(0) - 1

    def row_copy(blk, slot, r, n):
        row = dest_ref[r * n_tok + blk * tb + n]
        return pltpu.make_async_copy(y_hbm.at[pl.ds(row, 1)], ybuf.at[slot, pl.ds(r * tb + n, 1)],
                                     sem.at[slot])

    @pl.when(i == 0)
    def _():
        for r in range(TOP_K):
            def issue(n, _):
                row_copy(0, 0, r, n).start()
                return 0
            lax.fori_loop(0, tb, issue, 0, unroll=DMA_UNROLL)

    def step(slot, prefetch):
        pltpu.make_async_copy(y_hbm.at[pl.ds(0, TOP_K * tb)], ybuf.at[slot], sem.at[slot]).wait()
        if prefetch:
            for r in range(TOP_K):
                for n in range(tb):
                    row_copy(i + 1, 1 - slot, r, n).start()
        w = w_ref[...]
        lo, hi = _unpack_halves(ybuf[slot, 0:tb])
        r_lo, r_hi = w[:, 0:1] * lo, w[:, 0:1] * hi
        for r in range(1, TOP_K):
            lo, hi = _unpack_halves(ybuf[slot, r * tb:(r + 1) * tb])
            r_lo, r_hi = r_lo + w[:, r:r + 1] * lo, r_hi + w[:, r:r + 1] * hi
        ffn = jnp.concatenate([r_lo, r_hi], axis=1) + s_ref[...].astype(F32)
        v = alpha * x1_ref[...] + g2_ref[...] * ffn
        o_ref[...] = _layer_norm(v, lg_ref[...], lb_ref[...])

    for slot in range(2):
        @pl.when((i % 2 == slot) & (i < last))
        def _():
            step(slot, True)

    @pl.when(i == last)
    def _():
        step(last_slot, False)


def _final(dest, ys, w_tok, x1, shared, g2, ln_g, ln_b, seq, alpha):
    t, d = x1.shape
    tb = _pick(seq, COMBINE_TB)
    per_b = seq // tb
    grid_spec = pltpu.PrefetchScalarGridSpec(
        num_scalar_prefetch=1,
        grid=(t // tb,),
        in_specs=[pl.BlockSpec(memory_space=pl.ANY),
                  pl.BlockSpec((tb, TOP_K), lambda i, dest: (i, 0)),
                  pl.BlockSpec((tb, d), lambda i, dest: (i, 0)),
                  pl.BlockSpec((tb, d), lambda i, dest: (i, 0)),
                  pl.BlockSpec((None, 1, d), lambda i, dest: (i // per_b, 0, 0)),
                  pl.BlockSpec((1, d), lambda i, dest: (0, 0)),
                  pl.BlockSpec((1, d), lambda i, dest: (0, 0))],
        out_specs=pl.BlockSpec((tb, d), lambda i, dest: (i, 0)),
        scratch_shapes=[pltpu.VMEM((2, TOP_K * tb, d // 2), jnp.uint32),
                        pltpu.SemaphoreType.DMA((2,))],
    )
    return pl.pallas_call(
        functools.partial(_final_kernel, tb=tb, n_tok=t, alpha=alpha, last_slot=(t // tb - 1) % 2),
        grid_spec=grid_spec,
        out_shape=jax.ShapeDtypeStruct((t, d), F32),
        compiler_params=_cparams(("arbitrary",), disable_bounds_checks=True),
        name="combine_ln2",
    )(dest, ys, w_tok, x1, shared, g2, ln_g, ln_b)


def _layer(x2, c, bsz, seq, w_ada, b_ada, w_in, b_forget, b_gate, w_branch_fox, w_branch_sb, w_out,
           ln1_g, ln1_b, w_router, router_bias, w_exp_gate, w_exp_up, w_exp_down,
           w_sh_gate, w_sh_up, w_sh_down, ln2_g, ln2_b, alpha):
    t, d = x2.shape
    hf = b_forget.shape[0]
    fw = w_branch_fox.shape[0]
    sw = w_branch_sb.shape[0]
    hs = sw // HEAD_DIM
    n_exp = w_router.shape[1]

    mod = _ada(c, w_ada, b_ada)
    sh1, sc1, g1, sh2, sc2, g2 = [m.reshape(bsz, 1, d) for m in jnp.split(mod, 6, axis=-1)]

    f0 = 3 * fw
    qkv_cols = 3 * fw + 3 * sw
    qs = HEAD_DIM ** -0.5 * LOG2E
    wt = w_in.T
    col_scale = jnp.concatenate([jnp.full((fw,), qs, F32), jnp.ones((2 * fw,), F32),
                                 jnp.full((sw,), qs, F32), jnp.ones((2 * sw + 2 * d,), F32)])
    w_packed = (jnp.concatenate([wt[:f0], wt[f0 + hf:]], axis=0) * col_scale[:, None]).astype(BF16)
    bias_packed = jnp.concatenate([jnp.zeros((qkv_cols,), F32), b_gate]).reshape(1, -1)
    wf = jnp.zeros((LANES, d), F32).at[:hf].set(wt[f0:f0 + hf]).astype(BF16)
    bf_pad = jnp.zeros((1, LANES), F32).at[0, :hf].set(b_forget)

    proj, fa = _inproj(x2, sc1, sh1, w_packed, bias_packed, wf, seq, qkv_cols)
    cum = _forget_cumsum(fa, bf_pad, seq)
    cum_rows = cum[:, :hf].reshape(bsz, seq, hf).transpose(0, 2, 1).reshape(bsz * hf, 1, seq)

    o_fox = _fox_attention(proj, cum_rows, bsz, seq, hf, 0, hf, 2 * hf)
    o_sb = _sb_attention(proj, bsz, seq, hs, 3 * hf, 3 * hf + hs, 3 * hf + 2 * hs)
    merged = _merge(o_fox, o_sb, w_branch_fox.astype(BF16), w_branch_sb.astype(BF16), proj,
                    qkv_cols, d)
    mix = _out_proj(merged, w_out.astype(BF16))
    x1, h2 = _ln1(mix, x2, g1, ln1_g.reshape(1, d), ln1_b.reshape(1, d), sc2, sh2, seq, alpha)

    rb = jnp.broadcast_to(router_bias.reshape(n_exp, 1), (n_exp, LANES))
    idx, rank, w_top, counts = _route(x1, sc2, sh2, w_router.T, rb, seq)
    tm = MOE_TM
    n_rows = t * TOP_K + (n_exp + 1) * tm
    dest2, be2, meta2 = _dest(idx, rank, counts, n_exp, tm, n_rows // tm)
    dest, block_e, meta = dest2.reshape(-1), be2[0, :n_rows // tm], meta2[0, :1]
    tok_buf = _invert(dest, t, n_rows)
    ys = _moe(tok_buf, block_e, meta, h2, w_exp_gate.astype(BF16), w_exp_up.astype(BF16),
              w_exp_down.astype(BF16), tm)
    shared = _shared(h2, w_sh_gate.astype(BF16), w_sh_up.astype(BF16), w_sh_down.astype(BF16))
    return _final(dest, ys, w_top.T, x1, shared, g2, ln2_g.reshape(1, d), ln2_b.reshape(1, d),
                  seq, alpha)


def kernel(x, c, w_ada, b_ada, w_in, b_forget, b_gate, w_branch_fox, w_branch_sb, w_out, ln1_g, ln1_b, w_router, router_bias, w_exp_gate, w_exp_up, w_exp_down, w_sh_gate, w_sh_up, w_sh_down, ln2_g, ln2_b):
    bsz, seq, d = x.shape
    depth = w_ada.shape[0]
    alpha = (2 * depth) ** 0.25
    x2 = x.reshape(bsz * seq, d)
    for l in range(depth):
        x2 = _layer(x2, c, bsz, seq, w_ada[l], b_ada[l], w_in[l], b_forget[l], b_gate[l],
                    w_branch_fox[l], w_branch_sb[l], w_out[l], ln1_g[l], ln1_b[l], w_router[l],
                    router_bias[l], w_exp_gate[l], w_exp_up[l], w_exp_down[l], w_sh_gate[l],
                    w_sh_up[l], w_sh_down[l], ln2_g[l], ln2_b[l], alpha)
    return x2.reshape(bsz, seq, d)
```

```python
import functools
import math

import jax
import jax.numpy as jnp
from jax import lax
from jax.experimental import pallas as pl
from jax.experimental.pallas import tpu as pltpu

F32 = jnp.float32
BF16 = jnp.bfloat16
I32 = jnp.int32

HEAD_DIM = 128
LANES = 128
N_GROUPS = 8
TOPK_GROUPS = 4
TOP_K = 8
ROUTED_SCALE = 2.5
LN_EPS = 1e-5
VMEM_LIMIT = 56 * 1024 * 1024
NEG_BIG = -1e30
LOG2E = 1.4426950408889634
FOX_TQ, FOX_TK, FOX_HP = 512, 512, 2
SB_TQ, SB_TK, SB_HP = 1024, 256, 2
MOE_TM = 256
MOE_SLOTS = 3
COMBINE_TB = 128
DMA_UNROLL = 8


def _cparams(sem, **kw):
    return pltpu.CompilerParams(dimension_semantics=sem, vmem_limit_bytes=VMEM_LIMIT, **kw)


def _pick(n, pref):
    t = min(n, pref)
    while n % t:
        t //= 2
    return t


def _dot(a, b):
    return jnp.dot(a, b, preferred_element_type=F32)


def _dot_nt(a, b, precision=None):
    return lax.dot_general(a, b, (((1,), (1,)), ((), ())), preferred_element_type=F32,
                           precision=precision)


def _ada_kernel(c_ref, w_ref, b_ref, o_ref):
    c = c_ref[...]
    s = c * (1.0 / (1.0 + jnp.exp(-c)))
    o_ref[...] = jnp.dot(s, w_ref[...], preferred_element_type=F32,
                         precision=lax.Precision.HIGHEST) + b_ref[...]


def _ada(c, w_ada, b_ada):
    bsz, d = c.shape
    n = w_ada.shape[1]
    rows = 8
    c_pad = jnp.zeros((rows, d), F32).at[:bsz].set(c)
    tn = _pick(n, 512)
    out = pl.pallas_call(
        _ada_kernel,
        grid=(n // tn,),
        in_specs=[pl.BlockSpec((rows, d), lambda j: (0, 0)),
                  pl.BlockSpec((d, tn), lambda j: (0, j)),
                  pl.BlockSpec((1, tn), lambda j: (0, j))],
        out_specs=pl.BlockSpec((rows, tn), lambda j: (0, j)),
        out_shape=jax.ShapeDtypeStruct((rows, n), F32),
        compiler_params=_cparams(("parallel",)),
        name="ada_mod",
    )(c_pad, w_ada, b_ada.reshape(1, n))
    return out[:bsz]


def _inproj_kernel(x_ref, sc_ref, sh_ref, w_ref, b_ref, wf_ref, o_ref, f_ref, h_ref, *, n_plain):
    j = pl.program_id(1)

    @pl.when(j == 0)
    def _():
        h = (x_ref[...] * (1.0 + sc_ref[...]) + sh_ref[...]).astype(BF16)
        h_ref[...] = h
        f_ref[...] = _dot_nt(h, wf_ref[...])

    acc = _dot_nt(h_ref[...], w_ref[...])

    @pl.when(j < n_plain)
    def _():
        o_ref[...] = acc.astype(o_ref.dtype)

    @pl.when(j >= n_plain)
    def _():
        z = acc + b_ref[...]
        o_ref[...] = (1.0 / (1.0 + jnp.exp(-z))).astype(o_ref.dtype)


def _inproj(x2, sc, sh, w_packed, bias_packed, wf, seq, n_plain_cols, skip_at, skip):
    t, d = x2.shape
    n = w_packed.shape[0] - skip
    tm = _pick(seq, 1024)
    tn = _pick(n_plain_cols, 512)
    while (n - n_plain_cols) % tn or skip_at % tn:
        tn //= 2
    per_b = seq // tm
    row_align = math.gcd(tn, skip)

    def w_rows(i, j):
        return pl.multiple_of(jnp.where(j < skip_at // tn, j * tn, j * tn + skip), row_align), 0
    return pl.pallas_call(
        functools.partial(_inproj_kernel, n_plain=n_plain_cols // tn),
        grid=(t // tm, n // tn),
        in_specs=[pl.BlockSpec((tm, d), lambda i, j: (i, 0), pipeline_mode=pl.Buffered(1)),
                  pl.BlockSpec((None, 1, d), lambda i, j: (i // per_b, 0, 0)),
                  pl.BlockSpec((None, 1, d), lambda i, j: (i // per_b, 0, 0)),
                  pl.BlockSpec((pl.Element(tn), pl.Element(d)), w_rows),
                  pl.BlockSpec((1, tn), lambda i, j: (0, j)),
                  pl.BlockSpec((LANES, d), lambda i, j: (0, 0))],
        out_specs=[pl.BlockSpec((tm, tn), lambda i, j: (i, j)),
                   pl.BlockSpec((tm, LANES), lambda i, j: (i, 0))],
        out_shape=[jax.ShapeDtypeStruct((t, n), BF16),
                   jax.ShapeDtypeStruct((t, LANES), F32)],
        scratch_shapes=[pltpu.VMEM((tm, d), BF16)],
        compiler_params=_cparams(("parallel", "arbitrary")),
        name="in_proj",
    )(x2, sc, sh, w_packed, bias_packed, wf)


def _cum_kernel(f_ref, b_ref, o_ref, carry_ref, *, per_b):
    i = pl.program_id(0)

    @pl.when(i % per_b == 0)
    def _():
        carry_ref[...] = jnp.zeros_like(carry_ref)

    z = f_ref[...] + b_ref[...]
    lf = jnp.minimum(z, 0.0) - jnp.log(1.0 + jnp.exp(-jnp.abs(z)))
    tb = lf.shape[0]
    row = lax.broadcasted_iota(I32, (tb, tb), 0)
    col = lax.broadcasted_iota(I32, (tb, tb), 1)
    tri = jnp.where(col <= row, 1.0, 0.0).astype(F32)
    cum = jnp.dot(tri, lf, preferred_element_type=F32,
                  precision=lax.Precision.HIGHEST) + carry_ref[...]
    o_ref[...] = cum
    carry_ref[...] = cum[tb - 1:tb, :]


def _forget_cumsum(fa, bf_pad, seq):
    t = fa.shape[0]
    tb = _pick(seq, 512)
    return pl.pallas_call(
        functools.partial(_cum_kernel, per_b=seq // tb),
        grid=(t // tb,),
        in_specs=[pl.BlockSpec((tb, LANES), lambda i: (i, 0)),
                  pl.BlockSpec((1, LANES), lambda i: (0, 0))],
        out_specs=pl.BlockSpec((tb, LANES), lambda i: (i, 0)),
        out_shape=jax.ShapeDtypeStruct((t, LANES), F32),
        scratch_shapes=[pltpu.VMEM((1, LANES), F32)],
        compiler_params=_cparams(("arbitrary",)),
        name="forget_cumsum",
    )(fa, bf_pad)


def _hs(h):
    return slice(h * HEAD_DIM, (h + 1) * HEAD_DIM)


def _fox_kernel(q_ref, k_ref, v_ref, ck_ref, o_ref, *, tq, tk, hp):
    i = pl.program_id(2)
    q0 = pl.multiple_of(i * tq, tq)
    heads = range(hp)
    n_diag = tq // tk
    qs = [q_ref[:, _hs(h)] for h in heads]
    c0s = [ck_ref[h, :, pl.ds(q0, tk)][:, 0:1] for h in heads]

    def scores(h, k0):
        s = _dot_nt(qs[h], k_ref[pl.ds(k0, tk), _hs(h)])
        return s + (c0s[h] - ck_ref[h, :, pl.ds(k0, tk)]) * LOG2E

    def update(h, s, k0, m, l, acc):
        m_new = jnp.maximum(m, jnp.max(s, axis=1, keepdims=True))
        p = jnp.exp2(s - m_new)
        alpha = jnp.exp2(m - m_new)
        l = alpha * l + jnp.sum(p, axis=1, keepdims=True)
        acc = alpha * acc + _dot(p.astype(BF16), v_ref[pl.ds(k0, tk), _hs(h)])
        return m_new, l, acc

    def body(j, carry):
        carry = list(carry)
        for u in range(n_diag):
            k0 = pl.multiple_of((j * n_diag + u) * tk, tk)
            for h in heads:
                carry[h] = update(h, scores(h, k0), k0, *carry[h])
        return tuple(carry)

    init = tuple((jnp.full((tq, 1), NEG_BIG, F32), jnp.zeros((tq, 1), F32),
                  jnp.zeros((tq, HEAD_DIM), F32)) for _ in heads)
    carry = list(lax.fori_loop(0, i, body, init))
    row = lax.broadcasted_iota(I32, (tq, tk), 0)
    col = lax.broadcasted_iota(I32, (tq, tk), 1)
    for dblk in range(n_diag):
        off = dblk * tk
        k0 = pl.multiple_of(q0 + off, tk)
        for h in heads:
            s = jnp.where(col + off <= row, scores(h, k0), NEG_BIG)
            carry[h] = update(h, s, k0, *carry[h])
    for h in heads:
        _, l, acc = carry[h]
        o_ref[:, _hs(h)] = (acc / l).astype(o_ref.dtype)


def _fox_attention(proj, cum_rows, bsz, seq, heads, q_blk, k_blk, v_blk):
    tq = _pick(seq, FOX_TQ)
    tk = _pick(tq, FOX_TK)
    hp = _pick(heads, FOX_HP)
    nq = seq // tq
    wd = hp * HEAD_DIM
    return pl.pallas_call(
        functools.partial(_fox_kernel, tq=tq, tk=tk, hp=hp),
        grid=(bsz, heads // hp, nq),
        in_specs=[pl.BlockSpec((tq, wd), lambda b, h, i: (b * nq + i, q_blk // hp + h)),
                  pl.BlockSpec((seq, wd), lambda b, h, i: (b, k_blk // hp + h)),
                  pl.BlockSpec((seq, wd), lambda b, h, i: (b, v_blk // hp + h)),
                  pl.BlockSpec((hp, 1, seq), lambda b, h, i: (b * (heads // hp) + h, 0, 0))],
        out_specs=pl.BlockSpec((tq, wd), lambda b, h, i: (b * nq + i, h)),
        out_shape=jax.ShapeDtypeStruct((bsz * seq, heads * HEAD_DIM), BF16),
        compiler_params=_cparams(("parallel", "parallel", "arbitrary")),
        name="fox_attention",
    )(proj, proj, proj, cum_rows)


def _sb_kernel(q_ref, k_ref, v_ref, o_ref, *, tq, tk, hp):
    i = pl.program_id(2)
    q0 = pl.multiple_of(i * tq, tq)
    heads = range(hp)
    n_diag = tq // tk
    r_io = lax.broadcasted_iota(I32, (tk, tk), 0)
    c_io = lax.broadcasted_iota(I32, (tk, tk), 1)
    not_before = jnp.where(r_io >= c_io, 1.0, 0.0).astype(BF16)
    row = lax.broadcasted_iota(I32, (tq, tk), 0)
    col = lax.broadcasted_iota(I32, (tq, tk), 1)
    qs = [q_ref[:, _hs(h)] for h in heads]

    def tile(h, k0):
        z = _dot_nt(qs[h], k_ref[pl.ds(k0, tk), _hs(h)])
        sp = jnp.maximum(z, 0.0) + jnp.log2(1.0 + jnp.exp2(-jnp.abs(z)))
        return z, sp

    runs = [jnp.zeros((tq, 1), F32) for _ in heads]
    accs = [jnp.zeros((tq, HEAD_DIM), F32) for _ in heads]
    for dblk in range(n_diag):
        off = (n_diag - 1 - dblk) * tk
        valid = col + off < row
        k0 = pl.multiple_of(q0 + off, tk)
        for h in heads:
            z, sp = tile(h, k0)
            sp = jnp.where(valid, sp, 0.0)
            suffix = _dot(sp.astype(BF16), not_before)
            w = jnp.where(valid, jnp.exp2(z - suffix - runs[h]), 0.0)
            accs[h] = accs[h] + _dot(w.astype(BF16), v_ref[pl.ds(k0, tk), _hs(h)])
            runs[h] = runs[h] + suffix[:, 0:1]

    def body(n, carry):
        runs, accs = list(carry[0]), list(carry[1])
        for u in range(n_diag):
            k0 = pl.multiple_of(q0 - (n * n_diag + u + 1) * tk, tk)
            for h in heads:
                z, sp = tile(h, k0)
                suffix = _dot(sp.astype(BF16), not_before)
                w = jnp.exp2(z - suffix - runs[h])
                accs[h] = accs[h] + _dot(w.astype(BF16), v_ref[pl.ds(k0, tk), _hs(h)])
                runs[h] = runs[h] + suffix[:, 0:1]
        return tuple(runs), tuple(accs)

    _, accs = lax.fori_loop(0, i, body, (tuple(runs), tuple(accs)))
    for h in heads:
        o_ref[:, _hs(h)] = accs[h].astype(o_ref.dtype)


def _sb_attention(proj, bsz, seq, heads, q_blk, k_blk, v_blk):
    tq = _pick(seq, SB_TQ)
    tk = _pick(tq, SB_TK)
    hp = _pick(heads, SB_HP)
    nq = seq // tq
    wd = hp * HEAD_DIM
    return pl.pallas_call(
        functools.partial(_sb_kernel, tq=tq, tk=tk, hp=hp),
        grid=(bsz, heads // hp, nq),
        in_specs=[pl.BlockSpec((tq, wd), lambda b, h, i: (b * nq + i, q_blk // hp + h)),
                  pl.BlockSpec((seq, wd), lambda b, h, i: (b, k_blk // hp + h)),
                  pl.BlockSpec((seq, wd), lambda b, h, i: (b, v_blk // hp + h))],
        out_specs=pl.BlockSpec((tq, wd), lambda b, h, i: (b * nq + i, h)),
        out_shape=jax.ShapeDtypeStruct((bsz * seq, heads * HEAD_DIM), BF16),
        compiler_params=_cparams(("parallel", "parallel", "arbitrary")),
        name="sb_attention",
    )(proj, proj, proj)


def _merge_kernel(of_ref, os_ref, wa_ref, wb_ref, gf_ref, gs_ref, o_ref):
    yf = _dot(of_ref[...], wa_ref[...])
    ys = _dot(os_ref[...], wb_ref[...])
    o_ref[...] = (gf_ref[...].astype(F32) * yf + gs_ref[...].astype(F32) * ys).astype(o_ref.dtype)


def _merge(o_fox, o_sb, wa, wb, proj, g_col0, d):
    t = o_fox.shape[0]
    tm = _pick(t, 512)
    tn = _pick(d, 1024)
    gf0 = g_col0 // tn
    gs0 = (g_col0 + d) // tn
    return pl.pallas_call(
        _merge_kernel,
        grid=(t // tm, d // tn),
        in_specs=[pl.BlockSpec((tm, o_fox.shape[1]), lambda i, j: (i, 0)),
                  pl.BlockSpec((tm, o_sb.shape[1]), lambda i, j: (i, 0)),
                  pl.BlockSpec((wa.shape[0], tn), lambda i, j: (0, j)),
                  pl.BlockSpec((wb.shape[0], tn), lambda i, j: (0, j)),
                  pl.BlockSpec((tm, tn), lambda i, j: (i, gf0 + j)),
                  pl.BlockSpec((tm, tn), lambda i, j: (i, gs0 + j))],
        out_specs=pl.BlockSpec((tm, tn), lambda i, j: (i, j)),
        out_shape=jax.ShapeDtypeStruct((t, d), BF16),
        compiler_params=_cparams(("parallel", "arbitrary")),
        name="branch_merge",
    )(o_fox, o_sb, wa, wb, proj, proj)


def _layer_norm(v, g, b):
    mu = jnp.mean(v, axis=-1, keepdims=True)
    c = v - mu
    var = jnp.mean(c * c, axis=-1, keepdims=True)
    return c * lax.rsqrt(var + LN_EPS) * g + b


def _pack_halves(a):
    half = a.shape[1] // 2
    lo = lax.bitcast_convert_type(a[:, :half].astype(BF16).astype(F32), jnp.uint32)
    hi = lax.bitcast_convert_type(a[:, half:].astype(BF16).astype(F32), jnp.uint32)
    return (hi & jnp.uint32(0xFFFF0000)) | (lo >> 16)


def _unpack_halves(p):
    lo = lax.bitcast_convert_type(p << 16, F32)
    hi = lax.bitcast_convert_type(p & jnp.uint32(0xFFFF0000), F32)
    return lo, hi


def _matmul_kernel(a_ref, w_ref, o_ref):
    o_ref[...] = _dot(a_ref[...], w_ref[...]).astype(o_ref.dtype)


def _out_proj(merged, w_out):
    t, d = merged.shape
    n = w_out.shape[1]
    tm = _pick(t, 1024)
    tn = _pick(n, 1024)
    return pl.pallas_call(
        _matmul_kernel,
        grid=(t // tm, n // tn),
        in_specs=[pl.BlockSpec((tm, d), lambda i, j: (i, 0)),
                  pl.BlockSpec((d, tn), lambda i, j: (0, j))],
        out_specs=pl.BlockSpec((tm, tn), lambda i, j: (i, j)),
        out_shape=jax.ShapeDtypeStruct((t, n), BF16),
        compiler_params=_cparams(("parallel", "arbitrary")),
        name="out_proj",
    )(merged, w_out)


def _ln1_kernel(y_ref, x_ref, g1_ref, lg_ref, lb_ref, sc_ref, sh_ref, x1_ref, h2_ref, *, alpha):
    v = alpha * x_ref[...] + g1_ref[...] * y_ref[...].astype(F32)
    x1 = _layer_norm(v, lg_ref[...], lb_ref[...])
    x1_ref[...] = x1
    h2_ref[...] = _pack_halves(x1 * (1.0 + sc_ref[...]) + sh_ref[...])


def _ln1(mix, x2, g1, ln_g, ln_b, sc2, sh2, seq, alpha):
    t, d = x2.shape
    tm = _pick(seq, 256)
    per_b = seq // tm
    row = lambda i: (i, 0)
    per_batch = lambda i: (i // per_b, 0, 0)
    const = lambda i: (0, 0)
    return pl.pallas_call(
        functools.partial(_ln1_kernel, alpha=alpha),
        grid=(t // tm,),
        in_specs=[pl.BlockSpec((tm, d), row),
                  pl.BlockSpec((tm, d), row),
                  pl.BlockSpec((None, 1, d), per_batch),
                  pl.BlockSpec((1, d), const),
                  pl.BlockSpec((1, d), const),
                  pl.BlockSpec((None, 1, d), per_batch),
                  pl.BlockSpec((None, 1, d), per_batch)],
        out_specs=[pl.BlockSpec((tm, d), row), pl.BlockSpec((tm, d // 2), row)],
        out_shape=[jax.ShapeDtypeStruct((t, d), F32), jax.ShapeDtypeStruct((t, d // 2), jnp.uint32)],
        compiler_params=_cparams(("parallel",)),
        name="residual_ln1",
    )(mix, x2, g1, ln_g, ln_b, sc2, sh2)


def _route_kernel(x1_ref, sc_ref, sh_ref, wr_ref, rb_ref, idx_ref, rank_ref, w_ref, cnt_ref, carry_ref,
                  *, n_exp):
    i = pl.program_id(0)

    @pl.when(i == 0)
    def _():
        carry_ref[...] = jnp.zeros_like(carry_ref)

    tb = x1_ref.shape[0]
    gsz = n_exp // N_GROUPS
    h = x1_ref[...] * (1.0 + sc_ref[...]) + sh_ref[...]
    logits = _dot_nt(wr_ref[...], h, precision=lax.Precision.HIGHEST)
    scores = 1.0 / (1.0 + jnp.exp(-logits))
    choice = scores + rb_ref[...][:, 0:1]
    neg = -jnp.inf

    g = choice.reshape(N_GROUPS, gsz, tb)
    io = lax.broadcasted_iota(I32, g.shape, 1)
    m1 = jnp.max(g, axis=1, keepdims=True)
    first = jnp.min(jnp.where(g == m1, io, gsz), axis=1, keepdims=True)
    m2 = jnp.max(jnp.where(io == first, neg, g), axis=1, keepdims=True)
    gs = (m1 + m2).reshape(N_GROUPS, tb)

    gio = lax.broadcasted_iota(I32, gs.shape, 0)
    gsel = jnp.zeros(gs.shape, F32)
    for _ in range(TOPK_GROUPS):
        m = jnp.max(gs, axis=0, keepdims=True)
        f = jnp.min(jnp.where(gs == m, gio, N_GROUPS), axis=0, keepdims=True)
        hit = gio == f
        gsel = jnp.where(hit, 1.0, gsel)
        gs = jnp.where(hit, neg, gs)
    emask = jnp.broadcast_to(gsel.reshape(N_GROUPS, 1, tb), (N_GROUPS, gsz, tb)).reshape(n_exp, tb)

    v = jnp.where(emask > 0.5, choice, neg)
    eio = lax.broadcasted_iota(I32, v.shape, 0)
    sel = jnp.zeros(v.shape, F32)
    idx_rows, w_rows = [], []
    for _ in range(TOP_K):
        m = jnp.max(v, axis=0, keepdims=True)
        f = jnp.min(jnp.where(v == m, eio, n_exp), axis=0, keepdims=True)
        hit = eio == f
        idx_rows.append(f)
        w_rows.append(jnp.sum(jnp.where(hit, scores, 0.0), axis=0, keepdims=True))
        sel = jnp.where(hit, 1.0, sel)
        v = jnp.where(hit, neg, v)
    wsum = w_rows[0]
    for r in range(1, TOP_K):
        wsum = wsum + w_rows[r]
    inv = 1.0 / (wsum + 1e-20)

    r_io = lax.broadcasted_iota(I32, (tb, tb), 0)
    c_io = lax.broadcasted_iota(I32, (tb, tb), 1)
    before = jnp.where(r_io < c_io, 1.0, 0.0).astype(BF16)
    excl = _dot(sel.astype(BF16), before) + carry_ref[...][:, 0:1]
    for r in range(TOP_K):
        hit = eio == idx_rows[r]
        rank = jnp.sum(jnp.where(hit, excl, 0.0), axis=0, keepdims=True)
        idx_ref[r:r + 1, :] = idx_rows[r]
        rank_ref[r:r + 1, :] = rank.astype(I32)
        w_ref[r:r + 1, :] = w_rows[r] * inv * ROUTED_SCALE
    total = carry_ref[...] + jnp.sum(sel, axis=1, keepdims=True)
    carry_ref[...] = total
    cnt_ref[...] = total.astype(I32)


def _route(x1, sc2, sh2, wr_t, rb, seq):
    t, d = x1.shape
    n_exp = wr_t.shape[0]
    tb = _pick(seq, 512)
    per_b = seq // tb
    blk = pl.BlockSpec((TOP_K, tb), lambda i: (0, i))
    return pl.pallas_call(
        functools.partial(_route_kernel, n_exp=n_exp),
        grid=(t // tb,),
        in_specs=[pl.BlockSpec((tb, d), lambda i: (i, 0)),
                  pl.BlockSpec((None, 1, d), lambda i: (i // per_b, 0, 0)),
                  pl.BlockSpec((None, 1, d), lambda i: (i // per_b, 0, 0)),
                  pl.BlockSpec((n_exp, d), lambda i: (0, 0)),
                  pl.BlockSpec((n_exp, LANES), lambda i: (0, 0))],
        out_specs=[blk, blk, blk, pl.BlockSpec((n_exp, LANES), lambda i: (0, 0))],
        out_shape=[jax.ShapeDtypeStruct((TOP_K, t), I32),
                   jax.ShapeDtypeStruct((TOP_K, t), I32),
                   jax.ShapeDtypeStruct((TOP_K, t), F32),
                   jax.ShapeDtypeStruct((n_exp, LANES), I32)],
        scratch_shapes=[pltpu.VMEM((n_exp, LANES), F32)],
        compiler_params=_cparams(("arbitrary",)),
        name="router_topk",
    )(x1, sc2, sh2, wr_t, rb)


def _dest_kernel(idx_ref, rank_ref, cnt_ref, dest_ref, be_ref, meta_ref, *, n_exp, tm, nbp):
    cnt = cnt_ref[...]
    nblk = jnp.right_shift(cnt + (tm - 1), tm.bit_length() - 1)
    r_io = lax.broadcasted_iota(I32, (n_exp, n_exp), 0)
    c_io = lax.broadcasted_iota(I32, (n_exp, n_exp), 1)
    below = jnp.where(c_io < r_io, 1.0, 0.0).astype(F32)
    start_blk = jnp.dot(below, nblk.astype(F32), preferred_element_type=F32,
                        precision=lax.Precision.HIGHEST).astype(I32)
    end_blk = start_blk + nblk
    start_row = start_blk * tm
    idx = idx_ref[...]
    dest = rank_ref[...]
    for e in range(n_exp):
        dest = dest + jnp.where(idx == e, start_row[e:e + 1, 0:1], 0)
    dest_ref[...] = dest
    b_io = lax.broadcasted_iota(I32, (n_exp, nbp), 1)
    done = jnp.sum(jnp.where(b_io >= end_blk[:, 0:1], 1.0, 0.0), axis=0, keepdims=True)
    be_ref[...] = jnp.minimum(done.astype(I32), n_exp - 1)
    meta_ref[...] = end_blk[n_exp - 1:n_exp, :]


def _dest(idx, rank, counts, n_exp, tm, n_blocks):
    assert tm & (tm - 1) == 0
    t = idx.shape[1]
    tb = _pick(t, 2048)
    nbp = -(-n_blocks // LANES) * LANES
    blk = pl.BlockSpec((TOP_K, tb), lambda i: (0, i))
    return pl.pallas_call(
        functools.partial(_dest_kernel, n_exp=n_exp, tm=tm, nbp=nbp),
        grid=(t // tb,),
        in_specs=[blk, blk, pl.BlockSpec((n_exp, LANES), lambda i: (0, 0))],
        out_specs=[blk, pl.BlockSpec((1, nbp), lambda i: (0, 0)),
                   pl.BlockSpec((1, LANES), lambda i: (0, 0))],
        out_shape=[jax.ShapeDtypeStruct((TOP_K, t), I32),
                   jax.ShapeDtypeStruct((1, nbp), I32),
                   jax.ShapeDtypeStruct((1, LANES), I32)],
        compiler_params=_cparams(("arbitrary",)),
        name="dispatch_rows",
    )(idx, rank, counts)


def _invert_kernel(dest_ref, tok_ref, *, chunk, n_tok, n_rows):
    c = pl.program_id(0)

    @pl.when(c == 0)
    def _():
        def zero(r, _):
            tok_ref[r] = 0
            return 0
        lax.fori_loop(0, n_rows, zero, 0, unroll=DMA_UNROLL)

    tok0 = (c * chunk) % n_tok

    def place(n, _):
        tok_ref[dest_ref[n]] = tok0 + n
        return 0
    lax.fori_loop(0, chunk, place, 0, unroll=DMA_UNROLL)


def _invert(dest_flat, n_tok, n_rows):
    n_assign = dest_flat.shape[0]
    chunk = _pick(n_tok, 8192)
    return pl.pallas_call(
        functools.partial(_invert_kernel, chunk=chunk, n_tok=n_tok, n_rows=n_rows),
        grid=(n_assign // chunk,),
        in_specs=[pl.BlockSpec((chunk,), lambda c: (c,), memory_space=pltpu.SMEM)],
        out_specs=pl.BlockSpec(memory_space=pltpu.SMEM),
        out_shape=jax.ShapeDtypeStruct((n_rows,), I32),
        compiler_params=_cparams(("arbitrary",)),
        name="dispatch_tokens",
    )(dest_flat)


def _swiglu_packed(xp, wg_ref, wu_ref, wd_ref):
    lo, hi = _unpack_halves(xp)
    x = jnp.concatenate([lo.astype(BF16), hi.astype(BF16)], axis=1)
    g = _dot(x, wg_ref[...])
    u = _dot(x, wu_ref[...])
    a = (g * (1.0 / (1.0 + jnp.exp(-g))) * u).astype(BF16)
    return _dot(a, wd_ref[...])


def _moe_kernel(tok_ref, be_ref, meta_ref, h_hbm, wg_ref, wu_ref, wd_ref, y_ref, xbuf, xp_ref, sem,
                *, tm):
    b = pl.program_id(0)
    used = meta_ref[0]

    def row_copy(blk, slot, r):
        tok = tok_ref[blk * tm + r]
        return pltpu.make_async_copy(h_hbm.at[pl.ds(tok, 1)], xbuf.at[slot, pl.ds(r, 1)],
                                     sem.at[slot])

    def gather(blk, slot):
        def issue(r, _):
            row_copy(blk, slot, r).start()
            return 0
        lax.fori_loop(0, tm, issue, 0, unroll=DMA_UNROLL)

    @pl.when(b == 0)
    def _():
        gather(0, 0)
        gather(1, 1)

    @pl.when(b <= used + 1)
    def _():
        slot = b % MOE_SLOTS
        pltpu.make_async_copy(h_hbm.at[pl.ds(0, tm)], xbuf.at[slot], sem.at[slot]).wait()

    @pl.when(b < used)
    def _():
        xp_ref[...] = xbuf[b % MOE_SLOTS]
        nxt = (b + 2) % MOE_SLOTS
        for r in range(tm):
            row_copy(b + 2, nxt, r).start()
        y_ref[...] = _pack_halves(_swiglu_packed(xp_ref[...], wg_ref, wu_ref, wd_ref))

    @pl.when(b >= used)
    def _():
        y_ref[...] = jnp.zeros_like(y_ref)


def _moe(tok_buf, block_e, meta, h2p, wg, wu, wd, tm):
    n_rows = tok_buf.shape[0]
    dp = h2p.shape[1]
    d = 2 * dp
    f = wg.shape[2]
    grid_spec = pltpu.PrefetchScalarGridSpec(
        num_scalar_prefetch=3,
        grid=(n_rows // tm,),
        in_specs=[pl.BlockSpec(memory_space=pl.ANY),
                  pl.BlockSpec((None, d, f), lambda b, tok, be, meta: (be[b], 0, 0)),
                  pl.BlockSpec((None, d, f), lambda b, tok, be, meta: (be[b], 0, 0)),
                  pl.BlockSpec((None, f, d), lambda b, tok, be, meta: (be[b], 0, 0))],
        out_specs=pl.BlockSpec((tm, dp), lambda b, tok, be, meta: (b, 0)),
        scratch_shapes=[pltpu.VMEM((MOE_SLOTS, tm, dp), jnp.uint32), pltpu.VMEM((tm, dp), jnp.uint32),
                        pltpu.SemaphoreType.DMA((MOE_SLOTS,))],
    )
    return pl.pallas_call(
        functools.partial(_moe_kernel, tm=tm),
        grid_spec=grid_spec,
        out_shape=jax.ShapeDtypeStruct((n_rows, dp), jnp.uint32),
        compiler_params=_cparams(("arbitrary",), disable_bounds_checks=True),
        name="routed_experts",
    )(tok_buf, block_e, meta, h2p, wg, wu, wd)


def _shared_kernel(h_ref, wg_ref, wu_ref, wd_ref, o_ref):
    o_ref[...] = _swiglu_packed(h_ref[...], wg_ref, wu_ref, wd_ref).astype(o_ref.dtype)


def _shared(h2p, wg, wu, wd):
    t, dp = h2p.shape
    d = 2 * dp
    f = wg.shape[1]
    tm = _pick(t, 512)
    return pl.pallas_call(
        _shared_kernel,
        grid=(t // tm,),
        in_specs=[pl.BlockSpec((tm, dp), lambda i: (i, 0)),
                  pl.BlockSpec((d, f), lambda i: (0, 0)),
                  pl.BlockSpec((d, f), lambda i: (0, 0)),
                  pl.BlockSpec((f, d), lambda i: (0, 0))],
        out_specs=pl.BlockSpec((tm, d), lambda i: (i, 0)),
        out_shape=jax.ShapeDtypeStruct((t, d), BF16),
        compiler_params=_cparams(("parallel",)),
        name="shared_expert",
    )(h2p, wg, wu, wd)


def _final_kernel(dest_ref, y_hbm, w_ref, x1_ref, s_ref, g2_ref, lg_ref, lb_ref, o_ref, ybuf, sem,
                  *, tb, n_tok, alpha, last_slot):
    i = pl.program_id(0)
    last = pl.num_programs(0) - 1

    def row_copy(blk, slot, r, n):
        row = dest_ref[r * n_tok + blk * tb + n]
        return pltpu.make_async_copy(y_hbm.at[pl.ds(row, 1)], ybuf.at[slot, pl.ds(r * tb + n, 1)],
                                     sem.at[slot])

    @pl.when(i == 0)
    def _():
        for r in range(TOP_K):
            def issue(n, _):
                row_copy(0, 0, r, n).start()
                return 0
            lax.fori_loop(0, tb, issue, 0, unroll=DMA_UNROLL)

    def step(slot, prefetch):
        pltpu.make_async_copy(y_hbm.at[pl.ds(0, TOP_K * tb)], ybuf.at[slot], sem.at[slot]).wait()
        if prefetch:
            for r in range(TOP_K):
                for n in range(tb):
                    row_copy(i + 1, 1 - slot, r, n).start()
        w = w_ref[...]
        lo, hi = _unpack_halves(ybuf[slot, 0:tb])
        r_lo, r_hi = w[:, 0:1] * lo, w[:, 0:1] * hi
        for r in range(1, TOP_K):
            lo, hi = _unpack_halves(ybuf[slot, r * tb:(r + 1) * tb])
            r_lo, r_hi = r_lo + w[:, r:r + 1] * lo, r_hi + w[:, r:r + 1] * hi
        ffn = jnp.concatenate([r_lo, r_hi], axis=1) + s_ref[...].astype(F32)
        v = alpha * x1_ref[...] + g2_ref[...] * ffn
        o_ref[...] = _layer_norm(v, lg_ref[...], lb_ref[...])

    for slot in range(2):
        @pl.when((i % 2 == slot) & (i < last))
        def _():
            step(slot, True)

    @pl.when(i == last)
    def _():
        step(last_slot, False)


def _final(dest, ys, w_tok, x1, shared, g2, ln_g, ln_b, seq, alpha):
    t, d = x1.shape
    tb = _pick(seq, COMBINE_TB)
    per_b = seq // tb
    grid_spec = pltpu.PrefetchScalarGridSpec(
        num_scalar_prefetch=1,
        grid=(t // tb,),
        in_specs=[pl.BlockSpec(memory_space=pl.ANY),
                  pl.BlockSpec((tb, TOP_K), lambda i, dest: (i, 0)),
                  pl.BlockSpec((tb, d), lambda i, dest: (i, 0)),
                  pl.BlockSpec((tb, d), lambda i, dest: (i, 0)),
                  pl.BlockSpec((None, 1, d), lambda i, dest: (i // per_b, 0, 0)),
                  pl.BlockSpec((1, d), lambda i, dest: (0, 0)),
                  pl.BlockSpec((1, d), lambda i, dest: (0, 0))],
        out_specs=pl.BlockSpec((tb, d), lambda i, dest: (i, 0)),
        scratch_shapes=[pltpu.VMEM((2, TOP_K * tb, d // 2), jnp.uint32),
                        pltpu.SemaphoreType.DMA((2,))],
    )
    return pl.pallas_call(
        functools.partial(_final_kernel, tb=tb, n_tok=t, alpha=alpha, last_slot=(t // tb - 1) % 2),
        grid_spec=grid_spec,
        out_shape=jax.ShapeDtypeStruct((t, d), F32),
        compiler_params=_cparams(("arbitrary",), disable_bounds_checks=True),
        name="combine_ln2",
    )(dest, ys, w_tok, x1, shared, g2, ln_g, ln_b)


def _layer(x2, c, bsz, seq, w_ada, b_ada, w_in, b_forget, b_gate, w_branch_fox, w_branch_sb, w_out,
           ln1_g, ln1_b, w_router, router_bias, w_exp_gate, w_exp_up, w_exp_down,
           w_sh_gate, w_sh_up, w_sh_down, ln2_g, ln2_b, alpha):
    t, d = x2.shape
    hf = b_forget.shape[0]
    fw = w_branch_fox.shape[0]
    sw = w_branch_sb.shape[0]
    hs = sw // HEAD_DIM
    n_exp = w_router.shape[1]

    mod = _ada(c, w_ada, b_ada)
    sh1, sc1, g1, sh2, sc2, g2 = [m.reshape(bsz, 1, d) for m in jnp.split(mod, 6, axis=-1)]

    f0 = 3 * fw
    qkv_cols = 3 * fw + 3 * sw
    qs = HEAD_DIM ** -0.5 * LOG2E
    wt = w_in.T
    col_scale = jnp.concatenate([jnp.full((fw,), qs, F32), jnp.ones((2 * fw + hf,), F32),
                                 jnp.full((sw,), qs, F32), jnp.ones((2 * sw + 2 * d,), F32)])
    w_packed = (wt * col_scale[:, None]).astype(BF16)
    bias_packed = jnp.concatenate([jnp.zeros((qkv_cols,), F32), b_gate]).reshape(1, -1)
    wf = jnp.zeros((LANES, d), BF16).at[:hf].set(w_packed[f0:f0 + hf])
    bf_pad = jnp.zeros((1, LANES), F32).at[0, :hf].set(b_forget)

    proj, fa = _inproj(x2, sc1, sh1, w_packed, bias_packed, wf, seq, qkv_cols, f0, hf)
    cum = _forget_cumsum(fa, bf_pad, seq)
    cum_rows = cum[:, :hf].reshape(bsz, seq, hf).transpose(0, 2, 1).reshape(bsz * hf, 1, seq)

    o_fox = _fox_attention(proj, cum_rows, bsz, seq, hf, 0, hf, 2 * hf)
    o_sb = _sb_attention(proj, bsz, seq, hs, 3 * hf, 3 * hf + hs, 3 * hf + 2 * hs)
    merged = _merge(o_fox, o_sb, w_branch_fox.astype(BF16), w_branch_sb.astype(BF16), proj,
                    qkv_cols, d)
    mix = _out_proj(merged, w_out.astype(BF16))
    x1, h2 = _ln1(mix, x2, g1, ln1_g.reshape(1, d), ln1_b.reshape(1, d), sc2, sh2, seq, alpha)

    rb = jnp.broadcast_to(router_bias.reshape(n_exp, 1), (n_exp, LANES))
    idx, rank, w_top, counts = _route(x1, sc2, sh2, w_router.T, rb, seq)
    tm = MOE_TM
    n_rows = t * TOP_K + (n_exp + 1) * tm
    dest2, be2, meta2 = _dest(idx, rank, counts, n_exp, tm, n_rows // tm)
    dest, block_e, meta = dest2.reshape(-1), be2[0, :n_rows // tm], meta2[0, :1]
    tok_buf = _invert(dest, t, n_rows)
    ys = _moe(tok_buf, block_e, meta, h2, w_exp_gate.astype(BF16), w_exp_up.astype(BF16),
              w_exp_down.astype(BF16), tm)
    shared = _shared(h2, w_sh_gate.astype(BF16), w_sh_up.astype(BF16), w_sh_down.astype(BF16))
    return _final(dest, ys, w_top.T, x1, shared, g2, ln2_g.reshape(1, d), ln2_b.reshape(1, d),
                  seq, alpha)


def kernel(x, c, w_ada, b_ada, w_in, b_forget, b_gate, w_branch_fox, w_branch_sb, w_out, ln1_g, ln1_b, w_router, router_bias, w_exp_gate, w_exp_up, w_exp_down, w_sh_gate, w_sh_up, w_sh_down, ln2_g, ln2_b):
    bsz, seq, d = x.shape
    depth = w_ada.shape[0]
    alpha = (2 * depth) ** 0.25
    x2 = x.reshape(bsz * seq, d)
    for l in range(depth):
        x2 = _layer(x2, c, bsz, seq, w_ada[l], b_ada[l], w_in[l], b_forget[l], b_gate[l],
                    w_branch_fox[l], w_branch_sb[l], w_out[l], ln1_g[l], ln1_b[l], w_router[l],
                    router_bias[l], w_exp_gate[l], w_exp_up[l], w_exp_down[l], w_sh_gate[l],
                    w_sh_up[l], w_sh_down[l], ln2_g[l], ln2_b[l], alpha)
    return x2.reshape(bsz, seq, d)
```

```python
import functools
import math

import jax
import jax.numpy as jnp
from jax import lax
from jax.experimental import pallas as pl
from jax.experimental.pallas import tpu as pltpu

F32 = jnp.float32
BF16 = jnp.bfloat16
I32 = jnp.int32

HEAD_DIM = 128
LANES = 128
N_GROUPS = 8
TOPK_GROUPS = 4
TOP_K = 8
ROUTED_SCALE = 2.5
LN_EPS = 1e-5
VMEM_LIMIT = 56 * 1024 * 1024
NEG_BIG = -1e30
LOG2E = 1.4426950408889634
SOFTPLUS_LINEAR = 64.0
FOX_TQ, FOX_TK, FOX_HP = 512, 512, 2
SB_TQ, SB_TK, SB_HP = 1024, 256, 2
MOE_TM = 256
MOE_SLOTS = 3
COMBINE_TB = 128
DMA_UNROLL = 8


def _cparams(sem, **kw):
    return pltpu.CompilerParams(dimension_semantics=sem, vmem_limit_bytes=VMEM_LIMIT, **kw)


def _pick(n, pref):
    t = min(n, pref)
    while n % t:
        t //= 2
    return t


def _dot(a, b):
    return jnp.dot(a, b, preferred_element_type=F32)


def _dot_nt(a, b, precision=None):
    return lax.dot_general(a, b, (((1,), (1,)), ((), ())), preferred_element_type=F32,
                           precision=precision)


def _ada_kernel(c_ref, w_ref, b_ref, o_ref):
    c = c_ref[...]
    s = c * (1.0 / (1.0 + jnp.exp(-c)))
    o_ref[...] = jnp.dot(s, w_ref[...], preferred_element_type=F32,
                         precision=lax.Precision.HIGHEST) + b_ref[...]


def _ada(c, w_ada, b_ada):
    bsz, d = c.shape
    n = w_ada.shape[1]
    rows = 8
    c_pad = jnp.zeros((rows, d), F32).at[:bsz].set(c)
    tn = _pick(n, 512)
    out = pl.pallas_call(
        _ada_kernel,
        grid=(n // tn,),
        in_specs=[pl.BlockSpec((rows, d), lambda j: (0, 0)),
                  pl.BlockSpec((d, tn), lambda j: (0, j)),
                  pl.BlockSpec((1, tn), lambda j: (0, j))],
        out_specs=pl.BlockSpec((rows, tn), lambda j: (0, j)),
        out_shape=jax.ShapeDtypeStruct((rows, n), F32),
        compiler_params=_cparams(("parallel",)),
        name="ada_mod",
    )(c_pad, w_ada, b_ada.reshape(1, n))
    return out[:bsz]


def _inproj_kernel(x_ref, sc_ref, sh_ref, w_ref, b_ref, wf_ref, o_ref, f_ref, h_ref, *, n_plain):
    j = pl.program_id(1)

    @pl.when(j == 0)
    def _():
        h = (x_ref[...] * (1.0 + sc_ref[...]) + sh_ref[...]).astype(BF16)
        h_ref[...] = h
        f_ref[...] = _dot_nt(h, wf_ref[...])

    @pl.when(j < n_plain)
    def _():
        o_ref[...] = _dot_nt(h_ref[...], w_ref[...]).astype(o_ref.dtype)

    @pl.when(j >= n_plain)
    def _():
        z = _dot_nt(h_ref[...], w_ref[...]) + b_ref[...]
        o_ref[...] = (1.0 / (1.0 + jnp.exp(-z))).astype(o_ref.dtype)


def _inproj(x2, sc, sh, w_packed, bias_packed, wf, seq, n_plain_cols, skip_at, skip):
    t, d = x2.shape
    n = w_packed.shape[0] - skip
    tm = _pick(seq, 512)
    tn = _pick(n_plain_cols, 1024)
    while (n - n_plain_cols) % tn or skip_at % tn:
        tn //= 2
    per_b = seq // tm
    row_align = math.gcd(tn, skip)

    def w_rows(i, j):
        return pl.multiple_of(jnp.where(j < skip_at // tn, j * tn, j * tn + skip), row_align), 0
    return pl.pallas_call(
        functools.partial(_inproj_kernel, n_plain=n_plain_cols // tn),
        grid=(t // tm, n // tn),
        in_specs=[pl.BlockSpec((tm, d), lambda i, j: (i, 0)),
                  pl.BlockSpec((None, 1, d), lambda i, j: (i // per_b, 0, 0)),
                  pl.BlockSpec((None, 1, d), lambda i, j: (i // per_b, 0, 0)),
                  pl.BlockSpec((pl.Element(tn), pl.Element(d)), w_rows),
                  pl.BlockSpec((1, tn), lambda i, j: (0, j)),
                  pl.BlockSpec((LANES, d), lambda i, j: (0, 0))],
        out_specs=[pl.BlockSpec((tm, tn), lambda i, j: (i, j)),
                   pl.BlockSpec((tm, LANES), lambda i, j: (i, 0))],
        out_shape=[jax.ShapeDtypeStruct((t, n), BF16),
                   jax.ShapeDtypeStruct((t, LANES), F32)],
        scratch_shapes=[pltpu.VMEM((tm, d), BF16)],
        compiler_params=_cparams(("parallel", "arbitrary")),
        name="in_proj",
    )(x2, sc, sh, w_packed, bias_packed, wf)


def _cum_kernel(f_ref, b_ref, o_ref, carry_ref, *, per_b):
    i = pl.program_id(0)

    @pl.when(i % per_b == 0)
    def _():
        carry_ref[...] = jnp.zeros_like(carry_ref)

    z = f_ref[...] + b_ref[...]
    lf = jnp.minimum(z, 0.0) - jnp.log(1.0 + jnp.exp(-jnp.abs(z)))
    tb = lf.shape[0]
    row = lax.broadcasted_iota(I32, (tb, tb), 0)
    col = lax.broadcasted_iota(I32, (tb, tb), 1)
    tri = jnp.where(col <= row, 1.0, 0.0).astype(F32)
    cum = jnp.dot(tri, lf, preferred_element_type=F32,
                  precision=lax.Precision.HIGHEST) + carry_ref[...]
    o_ref[...] = cum
    carry_ref[...] = cum[tb - 1:tb, :]


def _forget_cumsum(fa, bf_pad, seq):
    t = fa.shape[0]
    tb = _pick(seq, 512)
    return pl.pallas_call(
        functools.partial(_cum_kernel, per_b=seq // tb),
        grid=(t // tb,),
        in_specs=[pl.BlockSpec((tb, LANES), lambda i: (i, 0)),
                  pl.BlockSpec((1, LANES), lambda i: (0, 0))],
        out_specs=pl.BlockSpec((tb, LANES), lambda i: (i, 0)),
        out_shape=jax.ShapeDtypeStruct((t, LANES), F32),
        scratch_shapes=[pltpu.VMEM((1, LANES), F32)],
        compiler_params=_cparams(("arbitrary",)),
        name="forget_cumsum",
    )(fa, bf_pad)


def _hs(h):
    return slice(h * HEAD_DIM, (h + 1) * HEAD_DIM)


def _fox_kernel(q_ref, k_ref, v_ref, ck_ref, o_ref, *, tq, tk, hp):
    i = pl.program_id(2)
    q0 = pl.multiple_of(i * tq, tq)
    heads = range(hp)
    n_diag = tq // tk
    qs = [q_ref[:, _hs(h)] for h in heads]
    c0s = [ck_ref[h, :, pl.ds(q0, tk)][:, 0:1] for h in heads]

    def scores(h, k0):
        s = _dot_nt(qs[h], k_ref[pl.ds(k0, tk), _hs(h)])
        return s + (c0s[h] - ck_ref[h, :, pl.ds(k0, tk)]) * LOG2E

    def update(h, s, k0, m, l, acc):
        m_new = jnp.maximum(m, jnp.max(s, axis=1, keepdims=True))
        p = jnp.exp2(s - m_new)
        alpha = jnp.exp2(m - m_new)
        l = alpha * l + jnp.sum(p, axis=1, keepdims=True)
        acc = alpha * acc + _dot(p.astype(BF16), v_ref[pl.ds(k0, tk), _hs(h)])
        return m_new, l, acc

    def body(j, carry):
        carry = list(carry)
        for u in range(n_diag):
            k0 = pl.multiple_of((j * n_diag + u) * tk, tk)
            for h in heads:
                carry[h] = update(h, scores(h, k0), k0, *carry[h])
        return tuple(carry)

    init = tuple((jnp.full((tq, 1), NEG_BIG, F32), jnp.zeros((tq, 1), F32),
                  jnp.zeros((tq, HEAD_DIM), F32)) for _ in heads)
    carry = list(lax.fori_loop(0, i, body, init))
    row = lax.broadcasted_iota(I32, (tq, tk), 0)
    col = lax.broadcasted_iota(I32, (tq, tk), 1)
    for dblk in range(n_diag):
        off = dblk * tk
        k0 = pl.multiple_of(q0 + off, tk)
        for h in heads:
            s = jnp.where(col + off <= row, scores(h, k0), NEG_BIG)
            carry[h] = update(h, s, k0, *carry[h])
    for h in heads:
        _, l, acc = carry[h]
        o_ref[:, _hs(h)] = (acc / l).astype(o_ref.dtype)


def _fox_attention(proj, cum_rows, bsz, seq, heads, q_blk, k_blk, v_blk):
    tq = _pick(seq, FOX_TQ)
    tk = _pick(tq, FOX_TK)
    hp = _pick(heads, FOX_HP)
    nq = seq // tq
    wd = hp * HEAD_DIM
    return pl.pallas_call(
        functools.partial(_fox_kernel, tq=tq, tk=tk, hp=hp),
        grid=(bsz, heads // hp, nq),
        in_specs=[pl.BlockSpec((tq, wd), lambda b, h, i: (b * nq + i, q_blk // hp + h)),
                  pl.BlockSpec((seq, wd), lambda b, h, i: (b, k_blk // hp + h)),
                  pl.BlockSpec((seq, wd), lambda b, h, i: (b, v_blk // hp + h)),
                  pl.BlockSpec((hp, 1, seq), lambda b, h, i: (b * (heads // hp) + h, 0, 0))],
        out_specs=pl.BlockSpec((tq, wd), lambda b, h, i: (b * nq + i, h)),
        out_shape=jax.ShapeDtypeStruct((bsz * seq, heads * HEAD_DIM), BF16),
        compiler_params=_cparams(("parallel", "parallel", "arbitrary")),
        name="fox_attention",
    )(proj, proj, proj, cum_rows)


def _sb_kernel(q_ref, k_ref, v_ref, o_ref, *, tq, tk, hp):
    i = pl.program_id(2)
    q0 = pl.multiple_of(i * tq, tq)
    heads = range(hp)
    n_diag = tq // tk
    r_io = lax.broadcasted_iota(I32, (tk, tk), 0)
    c_io = lax.broadcasted_iota(I32, (tk, tk), 1)
    not_before = jnp.where(r_io >= c_io, 1.0, 0.0).astype(BF16)
    row = lax.broadcasted_iota(I32, (tq, tk), 0)
    col = lax.broadcasted_iota(I32, (tq, tk), 1)
    qs = [q_ref[:, _hs(h)] for h in heads]

    def tile(h, k0):
        z = _dot_nt(qs[h], k_ref[pl.ds(k0, tk), _hs(h)])
        sp = jnp.where(z > SOFTPLUS_LINEAR, z, jnp.log2(1.0 + jnp.exp2(z)))
        return z, sp

    runs = [jnp.zeros((tq, 1), F32) for _ in heads]
    accs = [jnp.zeros((tq, HEAD_DIM), F32) for _ in heads]
    for dblk in range(n_diag):
        off = (n_diag - 1 - dblk) * tk
        valid = col + off < row
        k0 = pl.multiple_of(q0 + off, tk)
        for h in heads:
            z, sp = tile(h, k0)
            sp = jnp.where(valid, sp, 0.0)
            suffix = _dot(sp.astype(BF16), not_before)
            w = jnp.where(valid, jnp.exp2(z - suffix - runs[h]), 0.0)
            accs[h] = accs[h] + _dot(w.astype(BF16), v_ref[pl.ds(k0, tk), _hs(h)])
            runs[h] = runs[h] + suffix[:, 0:1]

    def body(n, carry):
        runs, accs = list(carry[0]), list(carry[1])
        for u in range(n_diag):
            k0 = pl.multiple_of(q0 - (n * n_diag + u + 1) * tk, tk)
            for h in heads:
                z, sp = tile(h, k0)
                suffix = _dot(sp.astype(BF16), not_before)
                w = jnp.exp2(z - suffix - runs[h])
                accs[h] = accs[h] + _dot(w.astype(BF16), v_ref[pl.ds(k0, tk), _hs(h)])
                runs[h] = runs[h] + suffix[:, 0:1]
        return tuple(runs), tuple(accs)

    _, accs = lax.fori_loop(0, i, body, (tuple(runs), tuple(accs)))
    for h in heads:
        o_ref[:, _hs(h)] = accs[h].astype(o_ref.dtype)


def _sb_attention(proj, bsz, seq, heads, q_blk, k_blk, v_blk):
    tq = _pick(seq, SB_TQ)
    tk = _pick(tq, SB_TK)
    hp = _pick(heads, SB_HP)
    nq = seq // tq
    wd = hp * HEAD_DIM
    return pl.pallas_call(
        functools.partial(_sb_kernel, tq=tq, tk=tk, hp=hp),
        grid=(bsz, heads // hp, nq),
        in_specs=[pl.BlockSpec((tq, wd), lambda b, h, i: (b * nq + i, q_blk // hp + h)),
                  pl.BlockSpec((seq, wd), lambda b, h, i: (b, k_blk // hp + h)),
                  pl.BlockSpec((seq, wd), lambda b, h, i: (b, v_blk // hp + h))],
        out_specs=pl.BlockSpec((tq, wd), lambda b, h, i: (b * nq + i, h)),
        out_shape=jax.ShapeDtypeStruct((bsz * seq, heads * HEAD_DIM), BF16),
        compiler_params=_cparams(("parallel", "parallel", "arbitrary")),
        name="sb_attention",
    )(proj, proj, proj)


def _merge_kernel(of_ref, os_ref, wa_ref, wb_ref, gf_ref, gs_ref, o_ref):
    yf = _dot(of_ref[...], wa_ref[...])
    ys = _dot(os_ref[...], wb_ref[...])
    o_ref[...] = (gf_ref[...].astype(F32) * yf + gs_ref[...].astype(F32) * ys).astype(o_ref.dtype)


def _merge(o_fox, o_sb, wa, wb, proj, g_col0, d):
    t = o_fox.shape[0]
    tm = _pick(t, 512)
    tn = _pick(d, 1024)
    gf0 = g_col0 // tn
    gs0 = (g_col0 + d) // tn
    return pl.pallas_call(
        _merge_kernel,
        grid=(t // tm, d // tn),
        in_specs=[pl.BlockSpec((tm, o_fox.shape[1]), lambda i, j: (i, 0)),
                  pl.BlockSpec((tm, o_sb.shape[1]), lambda i, j: (i, 0)),
                  pl.BlockSpec((wa.shape[0], tn), lambda i, j: (0, j)),
                  pl.BlockSpec((wb.shape[0], tn), lambda i, j: (0, j)),
                  pl.BlockSpec((tm, tn), lambda i, j: (i, gf0 + j)),
                  pl.BlockSpec((tm, tn), lambda i, j: (i, gs0 + j))],
        out_specs=pl.BlockSpec((tm, tn), lambda i, j: (i, j)),
        out_shape=jax.ShapeDtypeStruct((t, d), BF16),
        compiler_params=_cparams(("parallel", "arbitrary")),
        name="branch_merge",
    )(o_fox, o_sb, wa, wb, proj, proj)


def _layer_norm(v, g, b):
    mu = jnp.mean(v, axis=-1, keepdims=True)
    c = v - mu
    var = jnp.mean(c * c, axis=-1, keepdims=True)
    return c * lax.rsqrt(var + LN_EPS) * g + b


def _pack_halves(a):
    half = a.shape[1] // 2
    lo = lax.bitcast_convert_type(a[:, :half].astype(BF16).astype(F32), jnp.uint32)
    hi = lax.bitcast_convert_type(a[:, half:].astype(BF16).astype(F32), jnp.uint32)
    return (hi & jnp.uint32(0xFFFF0000)) | (lo >> 16)


def _unpack_halves(p):
    lo = lax.bitcast_convert_type(p << 16, F32)
    hi = lax.bitcast_convert_type(p & jnp.uint32(0xFFFF0000), F32)
    return lo, hi


def _matmul_kernel(a_ref, w_ref, o_ref):
    o_ref[...] = _dot(a_ref[...], w_ref[...]).astype(o_ref.dtype)


def _out_proj(merged, w_out):
    t, d = merged.shape
    n = w_out.shape[1]
    tm = _pick(t, 1024)
    tn = _pick(n, 1024)
    return pl.pallas_call(
        _matmul_kernel,
        grid=(t // tm, n // tn),
        in_specs=[pl.BlockSpec((tm, d), lambda i, j: (i, 0)),
                  pl.BlockSpec((d, tn), lambda i, j: (0, j))],
        out_specs=pl.BlockSpec((tm, tn), lambda i, j: (i, j)),
        out_shape=jax.ShapeDtypeStruct((t, n), BF16),
        compiler_params=_cparams(("parallel", "arbitrary")),
        name="out_proj",
    )(merged, w_out)


def _ln1_kernel(y_ref, x_ref, g1_ref, lg_ref, lb_ref, sc_ref, sh_ref, x1_ref, h2_ref, *, alpha):
    v = alpha * x_ref[...] + g1_ref[...] * y_ref[...].astype(F32)
    x1 = _layer_norm(v, lg_ref[...], lb_ref[...])
    x1_ref[...] = x1
    h2_ref[...] = _pack_halves(x1 * (1.0 + sc_ref[...]) + sh_ref[...])


def _ln1(mix, x2, g1, ln_g, ln_b, sc2, sh2, seq, alpha):
    t, d = x2.shape
    tm = _pick(seq, 256)
    per_b = seq // tm
    row = lambda i: (i, 0)
    per_batch = lambda i: (i // per_b, 0, 0)
    const = lambda i: (0, 0)
    return pl.pallas_call(
        functools.partial(_ln1_kernel, alpha=alpha),
        grid=(t // tm,),
        in_specs=[pl.BlockSpec((tm, d), row),
                  pl.BlockSpec((tm, d), row),
                  pl.BlockSpec((None, 1, d), per_batch),
                  pl.BlockSpec((1, d), const),
                  pl.BlockSpec((1, d), const),
                  pl.BlockSpec((None, 1, d), per_batch),
                  pl.BlockSpec((None, 1, d), per_batch)],
        out_specs=[pl.BlockSpec((tm, d), row), pl.BlockSpec((tm, d // 2), row)],
        out_shape=[jax.ShapeDtypeStruct((t, d), F32), jax.ShapeDtypeStruct((t, d // 2), jnp.uint32)],
        compiler_params=_cparams(("parallel",)),
        name="residual_ln1",
    )(mix, x2, g1, ln_g, ln_b, sc2, sh2)


def _route_kernel(x1_ref, sc_ref, sh_ref, wr_ref, rb_ref, idx_ref, rank_ref, w_ref, cnt_ref, carry_ref,
                  *, n_exp):
    i = pl.program_id(0)

    @pl.when(i == 0)
    def _():
        carry_ref[...] = jnp.zeros_like(carry_ref)

    tb = x1_ref.shape[0]
    gsz = n_exp // N_GROUPS
    h = x1_ref[...] * (1.0 + sc_ref[...]) + sh_ref[...]
    logits = _dot_nt(wr_ref[...], h, precision=lax.Precision.HIGHEST)
    scores = 1.0 / (1.0 + jnp.exp(-logits))
    choice = scores + rb_ref[...][:, 0:1]
    neg = -jnp.inf

    g = choice.reshape(N_GROUPS, gsz, tb)
    io = lax.broadcasted_iota(I32, g.shape, 1)
    m1 = jnp.max(g, axis=1, keepdims=True)
    first = jnp.min(jnp.where(g == m1, io, gsz), axis=1, keepdims=True)
    m2 = jnp.max(jnp.where(io == first, neg, g), axis=1, keepdims=True)
    gs = (m1 + m2).reshape(N_GROUPS, tb)

    gio = lax.broadcasted_iota(I32, gs.shape, 0)
    gsel = jnp.zeros(gs.shape, F32)
    for _ in range(TOPK_GROUPS):
        m = jnp.max(gs, axis=0, keepdims=True)
        f = jnp.min(jnp.where(gs == m, gio, N_GROUPS), axis=0, keepdims=True)
        hit = gio == f
        gsel = jnp.where(hit, 1.0, gsel)
        gs = jnp.where(hit, neg, gs)
    emask = jnp.broadcast_to(gsel.reshape(N_GROUPS, 1, tb), (N_GROUPS, gsz, tb)).reshape(n_exp, tb)

    v = jnp.where(emask > 0.5, choice, neg)
    eio = lax.broadcasted_iota(I32, v.shape, 0)
    sel = jnp.zeros(v.shape, F32)
    idx_rows, w_rows = [], []
    for _ in range(TOP_K):
        m = jnp.max(v, axis=0, keepdims=True)
        f = jnp.min(jnp.where(v == m, eio, n_exp), axis=0, keepdims=True)
        hit = eio == f
        idx_rows.append(f)
        w_rows.append(jnp.sum(jnp.where(hit, scores, 0.0), axis=0, keepdims=True))
        sel = jnp.where(hit, 1.0, sel)
        v = jnp.where(hit, neg, v)
    wsum = w_rows[0]
    for r in range(1, TOP_K):
        wsum = wsum + w_rows[r]
    inv = 1.0 / (wsum + 1e-20)

    r_io = lax.broadcasted_iota(I32, (tb, tb), 0)
    c_io = lax.broadcasted_iota(I32, (tb, tb), 1)
    before = jnp.where(r_io < c_io, 1.0, 0.0).astype(BF16)
    excl = _dot(sel.astype(BF16), before) + carry_ref[...][:, 0:1]
    for r in range(TOP_K):
        hit = eio == idx_rows[r]
        rank = jnp.sum(jnp.where(hit, excl, 0.0), axis=0, keepdims=True)
        idx_ref[r:r + 1, :] = idx_rows[r]
        rank_ref[r:r + 1, :] = rank.astype(I32)
        w_ref[r:r + 1, :] = w_rows[r] * inv * ROUTED_SCALE
    total = carry_ref[...] + jnp.sum(sel, axis=1, keepdims=True)
    carry_ref[...] = total
    cnt_ref[...] = total.astype(I32)


def _route(x1, sc2, sh2, wr_t, rb, seq):
    t, d = x1.shape
    n_exp = wr_t.shape[0]
    tb = _pick(seq, 512)
    per_b = seq // tb
    blk = pl.BlockSpec((TOP_K, tb), lambda i: (0, i))
    return pl.pallas_call(
        functools.partial(_route_kernel, n_exp=n_exp),
        grid=(t // tb,),
        in_specs=[pl.BlockSpec((tb, d), lambda i: (i, 0)),
                  pl.BlockSpec((None, 1, d), lambda i: (i // per_b, 0, 0)),
                  pl.BlockSpec((None, 1, d), lambda i: (i // per_b, 0, 0)),
                  pl.BlockSpec((n_exp, d), lambda i: (0, 0)),
                  pl.BlockSpec((n_exp, LANES), lambda i: (0, 0))],
        out_specs=[blk, blk, blk, pl.BlockSpec((n_exp, LANES), lambda i: (0, 0))],
        out_shape=[jax.ShapeDtypeStruct((TOP_K, t), I32),
                   jax.ShapeDtypeStruct((TOP_K, t), I32),
                   jax.ShapeDtypeStruct((TOP_K, t), F32),
                   jax.ShapeDtypeStruct((n_exp, LANES), I32)],
        scratch_shapes=[pltpu.VMEM((n_exp, LANES), F32)],
        compiler_params=_cparams(("arbitrary",)),
        name="router_topk",
    )(x1, sc2, sh2, wr_t, rb)


def _dest_kernel(idx_ref, rank_ref, cnt_ref, dest_ref, be_ref, meta_ref, *, n_exp, tm, nbp):
    cnt = cnt_ref[...]
    nblk = jnp.right_shift(cnt + (tm - 1), tm.bit_length() - 1)
    r_io = lax.broadcasted_iota(I32, (n_exp, n_exp), 0)
    c_io = lax.broadcasted_iota(I32, (n_exp, n_exp), 1)
    below = jnp.where(c_io < r_io, 1.0, 0.0).astype(F32)
    start_blk = jnp.dot(below, nblk.astype(F32), preferred_element_type=F32,
                        precision=lax.Precision.HIGHEST).astype(I32)
    end_blk = start_blk + nblk
    start_row = start_blk * tm
    idx = idx_ref[...]
    dest = rank_ref[...]
    for e in range(n_exp):
        dest = dest + jnp.where(idx == e, start_row[e:e + 1, 0:1], 0)
    dest_ref[...] = dest
    b_io = lax.broadcasted_iota(I32, (n_exp, nbp), 1)
    done = jnp.sum(jnp.where(b_io >= end_blk[:, 0:1], 1.0, 0.0), axis=0, keepdims=True)
    be_ref[...] = jnp.minimum(done.astype(I32), n_exp - 1)
    meta_ref[...] = end_blk[n_exp - 1:n_exp, :]


def _dest(idx, rank, counts, n_exp, tm, n_blocks):
    assert tm & (tm - 1) == 0
    t = idx.shape[1]
    tb = _pick(t, 2048)
    nbp = -(-n_blocks // LANES) * LANES
    blk = pl.BlockSpec((TOP_K, tb), lambda i: (0, i))
    return pl.pallas_call(
        functools.partial(_dest_kernel, n_exp=n_exp, tm=tm, nbp=nbp),
        grid=(t // tb,),
        in_specs=[blk, blk, pl.BlockSpec((n_exp, LANES), lambda i: (0, 0))],
        out_specs=[blk, pl.BlockSpec((1, nbp), lambda i: (0, 0)),
                   pl.BlockSpec((1, LANES), lambda i: (0, 0))],
        out_shape=[jax.ShapeDtypeStruct((TOP_K, t), I32),
                   jax.ShapeDtypeStruct((1, nbp), I32),
                   jax.ShapeDtypeStruct((1, LANES), I32)],
        compiler_params=_cparams(("arbitrary",)),
        name="dispatch_rows",
    )(idx, rank, counts)


def _invert_kernel(dest_ref, tok_ref, *, chunk, n_tok, n_rows):
    c = pl.program_id(0)

    @pl.when(c == 0)
    def _():
        def zero(r, _):
            tok_ref[r] = 0
            return 0
        lax.fori_loop(0, n_rows, zero, 0, unroll=DMA_UNROLL)

    tok0 = (c * chunk) % n_tok

    def place(n, _):
        tok_ref[dest_ref[n]] = tok0 + n
        return 0
    lax.fori_loop(0, chunk, place, 0, unroll=DMA_UNROLL)


def _invert(dest_flat, n_tok, n_rows):
    n_assign = dest_flat.shape[0]
    chunk = _pick(n_tok, 8192)
    return pl.pallas_call(
        functools.partial(_invert_kernel, chunk=chunk, n_tok=n_tok, n_rows=n_rows),
        grid=(n_assign // chunk,),
        in_specs=[pl.BlockSpec((chunk,), lambda c: (c,), memory_space=pltpu.SMEM)],
        out_specs=pl.BlockSpec(memory_space=pltpu.SMEM),
        out_shape=jax.ShapeDtypeStruct((n_rows,), I32),
        compiler_params=_cparams(("arbitrary",)),
        name="dispatch_tokens",
    )(dest_flat)


def _swiglu_packed(xp, wg_ref, wu_ref, wd_ref):
    lo, hi = _unpack_halves(xp)
    x = jnp.concatenate([lo.astype(BF16), hi.astype(BF16)], axis=1)
    g = _dot(x, wg_ref[...])
    u = _dot(x, wu_ref[...])
    a = (g * (1.0 / (1.0 + jnp.exp(-g))) * u).astype(BF16)
    return _dot(a, wd_ref[...])


def _moe_kernel(tok_ref, be_ref, meta_ref, h_hbm, wg_ref, wu_ref, wd_ref, y_ref, xbuf, xp_ref, sem,
                *, tm):
    b = pl.program_id(0)
    used = meta_ref[0]

    def row_copy(blk, slot, r):
        tok = tok_ref[blk * tm + r]
        return pltpu.make_async_copy(h_hbm.at[pl.ds(tok, 1)], xbuf.at[slot, pl.ds(r, 1)],
                                     sem.at[slot])

    def gather(blk, slot):
        def issue(r, _):
            row_copy(blk, slot, r).start()
            return 0
        lax.fori_loop(0, tm, issue, 0, unroll=DMA_UNROLL)

    @pl.when(b == 0)
    def _():
        gather(0, 0)
        gather(1, 1)

    @pl.when(b <= used + 1)
    def _():
        slot = b % MOE_SLOTS
        pltpu.make_async_copy(h_hbm.at[pl.ds(0, tm)], xbuf.at[slot], sem.at[slot]).wait()

    @pl.when(b < used)
    def _():
        xp_ref[...] = xbuf[b % MOE_SLOTS]
        nxt = (b + 2) % MOE_SLOTS
        for r in range(tm):
            row_copy(b + 2, nxt, r).start()
        y_ref[...] = _pack_halves(_swiglu_packed(xp_ref[...], wg_ref, wu_ref, wd_ref))

    @pl.when(b >= used)
    def _():
        y_ref[...] = jnp.zeros_like(y_ref)


def _moe(tok_buf, block_e, meta, h2p, wg, wu, wd, tm):
    n_rows = tok_buf.shape[0]
    dp = h2p.shape[1]
    d = 2 * dp
    f = wg.shape[2]
    grid_spec = pltpu.PrefetchScalarGridSpec(
        num_scalar_prefetch=3,
        grid=(n_rows // tm,),
        in_specs=[pl.BlockSpec(memory_space=pl.ANY),
                  pl.BlockSpec((None, d, f), lambda b, tok, be, meta: (be[b], 0, 0)),
                  pl.BlockSpec((None, d, f), lambda b, tok, be, meta: (be[b], 0, 0)),
                  pl.BlockSpec((None, f, d), lambda b, tok, be, meta: (be[b], 0, 0))],
        out_specs=pl.BlockSpec((tm, dp), lambda b, tok, be, meta: (b, 0)),
        scratch_shapes=[pltpu.VMEM((MOE_SLOTS, tm, dp), jnp.uint32), pltpu.VMEM((tm, dp), jnp.uint32),
                        pltpu.SemaphoreType.DMA((MOE_SLOTS,))],
    )
    return pl.pallas_call(
        functools.partial(_moe_kernel, tm=tm),
        grid_spec=grid_spec,
        out_shape=jax.ShapeDtypeStruct((n_rows, dp), jnp.uint32),
        compiler_params=_cparams(("arbitrary",), disable_bounds_checks=True),
        name="routed_experts",
    )(tok_buf, block_e, meta, h2p, wg, wu, wd)


def _shared_kernel(h_ref, wg_ref, wu_ref, wd_ref, o_ref):
    o_ref[...] = _swiglu_packed(h_ref[...], wg_ref, wu_ref, wd_ref).astype(o_ref.dtype)


def _shared(h2p, wg, wu, wd):
    t, dp = h2p.shape
    d = 2 * dp
    f = wg.shape[1]
    tm = _pick(t, 512)
    return pl.pallas_call(
        _shared_kernel,
        grid=(t // tm,),
        in_specs=[pl.BlockSpec((tm, dp), lambda i: (i, 0)),
                  pl.BlockSpec((d, f), lambda i: (0, 0)),
                  pl.BlockSpec((d, f), lambda i: (0, 0)),
                  pl.BlockSpec((f, d), lambda i: (0, 0))],
        out_specs=pl.BlockSpec((tm, d), lambda i: (i, 0)),
        out_shape=jax.ShapeDtypeStruct((t, d), BF16),
        compiler_params=_cparams(("parallel",)),
        name="shared_expert",
    )(h2p, wg, wu, wd)


def _final_kernel(dest_ref, y_hbm, w_ref, x1_ref, s_ref, g2_ref, lg_ref, lb_ref, o_ref, ybuf, sem,
                  *, tb, n_tok, alpha, last_slot):
    i = pl.program_id(0)
    last = pl.num_programs(0) - 1

    def row_copy(blk, slot, r, n):
        row = dest_ref[r * n_tok + blk * tb + n]
        return pltpu.make_async_copy(y_hbm.at[pl.ds(row, 1)], ybuf.at[slot, pl.ds(r * tb + n, 1)],
                                     sem.at[slot])

    @pl.when(i == 0)
    def _():
        for r in range(TOP_K):
            def issue(n, _):
                row_copy(0, 0, r, n).start()
                return 0
            lax.fori_loop(0, tb, issue, 0, unroll=DMA_UNROLL)

    def step(slot, prefetch):
        pltpu.make_async_copy(y_hbm.at[pl.ds(0, TOP_K * tb)], ybuf.at[slot], sem.at[slot]).wait()
        if prefetch:
            for r in range(TOP_K):
                for n in range(tb):
                    row_copy(i + 1, 1 - slot, r, n).start()
        w = w_ref[...]
        lo, hi = _unpack_halves(ybuf[slot, 0:tb])
        r_lo, r_hi = w[:, 0:1] * lo, w[:, 0:1] * hi
        for r in range(1, TOP_K):
            lo, hi = _unpack_halves(ybuf[slot, r * tb:(r + 1) * tb])
            r_lo, r_hi = r_lo + w[:, r:r + 1] * lo, r_hi + w[:, r:r + 1] * hi
        ffn = jnp.concatenate([r_lo, r_hi], axis=1) + s_ref[...].astype(F32)
        v = alpha * x1_ref[...] + g2_ref[...] * ffn
        o_ref[...] = _layer_norm(v, lg_ref[...], lb_ref[...])

    for slot in range(2):
        @pl.when((i % 2 == slot) & (i < last))
        def _():
            step(slot, True)

    @pl.when(i == last)
    def _():
        step(last_slot, False)


def _final(dest, ys, w_tok, x1, shared, g2, ln_g, ln_b, seq, alpha):
    t, d = x1.shape
    tb = _pick(seq, COMBINE_TB)
    per_b = seq // tb
    grid_spec = pltpu.PrefetchScalarGridSpec(
        num_scalar_prefetch=1,
        grid=(t // tb,),
        in_specs=[pl.BlockSpec(memory_space=pl.ANY),
                  pl.BlockSpec((tb, TOP_K), lambda i, dest: (i, 0)),
                  pl.BlockSpec((tb, d), lambda i, dest: (i, 0)),
                  pl.BlockSpec((tb, d), lambda i, dest: (i, 0)),
                  pl.BlockSpec((None, 1, d), lambda i, dest: (i // per_b, 0, 0)),
                  pl.BlockSpec((1, d), lambda i, dest: (0, 0)),
                  pl.BlockSpec((1, d), lambda i, dest: (0, 0))],
        out_specs=pl.BlockSpec((tb, d), lambda i, dest: (i, 0)),
        scratch_shapes=[pltpu.VMEM((2, TOP_K * tb, d // 2), jnp.uint32),
                        pltpu.SemaphoreType.DMA((2,))],
    )
    return pl.pallas_call(
        functools.partial(_final_kernel, tb=tb, n_tok=t, alpha=alpha, last_slot=(t // tb - 1) % 2),
        grid_spec=grid_spec,
        out_shape=jax.ShapeDtypeStruct((t, d), F32),
        compiler_params=_cparams(("arbitrary",), disable_bounds_checks=True),
        name="combine_ln2",
    )(dest, ys, w_tok, x1, shared, g2, ln_g, ln_b)


def _layer(x2, c, bsz, seq, w_ada, b_ada, w_in, b_forget, b_gate, w_branch_fox, w_branch_sb, w_out,
           ln1_g, ln1_b, w_router, router_bias, w_exp_gate, w_exp_up, w_exp_down,
           w_sh_gate, w_sh_up, w_sh_down, ln2_g, ln2_b, alpha):
    t, d = x2.shape
    hf = b_forget.shape[0]
    fw = w_branch_fox.shape[0]
    sw = w_branch_sb.shape[0]
    hs = sw // HEAD_DIM
    n_exp = w_router.shape[1]

    mod = _ada(c, w_ada, b_ada)
    sh1, sc1, g1, sh2, sc2, g2 = [m.reshape(bsz, 1, d) for m in jnp.split(mod, 6, axis=-1)]

    f0 = 3 * fw
    qkv_cols = 3 * fw + 3 * sw
    qs = HEAD_DIM ** -0.5 * LOG2E
    wt = w_in.T
    col_scale = jnp.concatenate([jnp.full((fw,), qs, F32), jnp.ones((2 * fw + hf,), F32),
                                 jnp.full((sw,), qs, F32), jnp.ones((2 * sw + 2 * d,), F32)])
    w_packed = (wt * col_scale[:, None]).astype(BF16)
    bias_packed = jnp.concatenate([jnp.zeros((qkv_cols,), F32), b_gate]).reshape(1, -1)
    wf = jnp.zeros((LANES, d), BF16).at[:hf].set(w_packed[f0:f0 + hf])
    bf_pad = jnp.zeros((1, LANES), F32).at[0, :hf].set(b_forget)

    proj, fa = _inproj(x2, sc1, sh1, w_packed, bias_packed, wf, seq, qkv_cols, f0, hf)
    cum = _forget_cumsum(fa, bf_pad, seq)
    cum_rows = cum[:, :hf].reshape(bsz, seq, hf).transpose(0, 2, 1).reshape(bsz * hf, 1, seq)

    o_fox = _fox_attention(proj, cum_rows, bsz, seq, hf, 0, hf, 2 * hf)
    o_sb = _sb_attention(proj, bsz, seq, hs, 3 * hf, 3 * hf + hs, 3 * hf + 2 * hs)
    merged = _merge(o_fox, o_sb, w_branch_fox.astype(BF16), w_branch_sb.astype(BF16), proj,
                    qkv_cols, d)
    mix = _out_proj(merged, w_out.astype(BF16))
    x1, h2 = _ln1(mix, x2, g1, ln1_g.reshape(1, d), ln1_b.reshape(1, d), sc2, sh2, seq, alpha)

    rb = jnp.broadcast_to(router_bias.reshape(n_exp, 1), (n_exp, LANES))
    idx, rank, w_top, counts = _route(x1, sc2, sh2, w_router.T, rb, seq)
    tm = MOE_TM
    n_rows = t * TOP_K + (n_exp + 1) * tm
    dest2, be2, meta2 = _dest(idx, rank, counts, n_exp, tm, n_rows // tm)
    dest, block_e, meta = dest2.reshape(-1), be2[0, :n_rows // tm], meta2[0, :1]
    tok_buf = _invert(dest, t, n_rows)
    ys = _moe(tok_buf, block_e, meta, h2, w_exp_gate.astype(BF16), w_exp_up.astype(BF16),
              w_exp_down.astype(BF16), tm)
    shared = _shared(h2, w_sh_gate.astype(BF16), w_sh_up.astype(BF16), w_sh_down.astype(BF16))
    return _final(dest, ys, w_top.T, x1, shared, g2, ln2_g.reshape(1, d), ln2_b.reshape(1, d),
                  seq, alpha)


def kernel(x, c, w_ada, b_ada, w_in, b_forget, b_gate, w_branch_fox, w_branch_sb, w_out, ln1_g, ln1_b, w_router, router_bias, w_exp_gate, w_exp_up, w_exp_down, w_sh_gate, w_sh_up, w_sh_down, ln2_g, ln2_b):
    bsz, seq, d = x.shape
    depth = w_ada.shape[0]
    alpha = (2 * depth) ** 0.25
    x2 = x.reshape(bsz * seq, d)
    for l in range(depth):
        x2 = _layer(x2, c, bsz, seq, w_ada[l], b_ada[l], w_in[l], b_forget[l], b_gate[l],
                    w_branch_fox[l], w_branch_sb[l], w_out[l], ln1_g[l], ln1_b[l], w_router[l],
                    router_bias[l], w_exp_gate[l], w_exp_up[l], w_exp_down[l], w_sh_gate[l],
                    w_sh_up[l], w_sh_down[l], ln2_g[l], ln2_b[l], alpha)
    return x2.reshape(bsz, seq, d)
```

```python
import functools
import math

import jax
import jax.numpy as jnp
from jax import lax
from jax.experimental import pallas as pl
from jax.experimental.pallas import tpu as pltpu

F32 = jnp.float32
BF16 = jnp.bfloat16
I32 = jnp.int32

HEAD_DIM = 128
LANES = 128
N_GROUPS = 8
TOPK_GROUPS = 4
TOP_K = 8
ROUTED_SCALE = 2.5
LN_EPS = 1e-5
VMEM_LIMIT = 56 * 1024 * 1024
NEG_BIG = -1e30
LOG2E = 1.4426950408889634
SOFTPLUS_LINEAR = 64.0
FOX_TQ, FOX_TK, FOX_HP = 1024, 1024, 1
SB_TQ, SB_TK, SB_HP = 1024, 256, 2
MOE_TM = 256
MOE_SLOTS = 3
COMBINE_TB = 128
DMA_UNROLL = 8


def _cparams(sem, **kw):
    return pltpu.CompilerParams(dimension_semantics=sem, vmem_limit_bytes=VMEM_LIMIT, **kw)


def _pick(n, pref):
    t = min(n, pref)
    while n % t:
        t //= 2
    return t


def _dot(a, b):
    return jnp.dot(a, b, preferred_element_type=F32)


def _dot_nt(a, b, precision=None):
    return lax.dot_general(a, b, (((1,), (1,)), ((), ())), preferred_element_type=F32,
                           precision=precision)


def _ada_kernel(c_ref, w_ref, b_ref, o_ref):
    c = c_ref[...]
    s = c * (1.0 / (1.0 + jnp.exp(-c)))
    o_ref[...] = jnp.dot(s, w_ref[...], preferred_element_type=F32,
                         precision=lax.Precision.HIGHEST) + b_ref[...]


def _ada(c, w_ada, b_ada):
    bsz, d = c.shape
    n = w_ada.shape[1]
    rows = 8
    c_pad = jnp.zeros((rows, d), F32).at[:bsz].set(c)
    tn = _pick(n, 512)
    out = pl.pallas_call(
        _ada_kernel,
        grid=(n // tn,),
        in_specs=[pl.BlockSpec((rows, d), lambda j: (0, 0)),
                  pl.BlockSpec((d, tn), lambda j: (0, j)),
                  pl.BlockSpec((1, tn), lambda j: (0, j))],
        out_specs=pl.BlockSpec((rows, tn), lambda j: (0, j)),
        out_shape=jax.ShapeDtypeStruct((rows, n), F32),
        compiler_params=_cparams(("parallel",)),
        name="ada_mod",
    )(c_pad, w_ada, b_ada.reshape(1, n))
    return out[:bsz]


def _inproj_kernel(x_ref, sc_ref, sh_ref, w_ref, b_ref, wf_ref, o_ref, f_ref, h_ref, *, n_plain):
    j = pl.program_id(1)

    @pl.when(j == 0)
    def _():
        h = (x_ref[...] * (1.0 + sc_ref[...]) + sh_ref[...]).astype(BF16)
        h_ref[...] = h
        f_ref[...] = _dot_nt(h, wf_ref[...])

    @pl.when(j < n_plain)
    def _():
        o_ref[...] = _dot_nt(h_ref[...], w_ref[...]).astype(o_ref.dtype)

    @pl.when(j >= n_plain)
    def _():
        z = _dot_nt(h_ref[...], w_ref[...]) + b_ref[...]
        o_ref[...] = (1.0 / (1.0 + jnp.exp(-z))).astype(o_ref.dtype)


def _inproj(x2, sc, sh, w_packed, bias_packed, wf, seq, n_plain_cols, skip_at, skip):
    t, d = x2.shape
    n = w_packed.shape[0] - skip
    tm = _pick(seq, 512)
    tn = _pick(n_plain_cols, 1024)
    while (n - n_plain_cols) % tn or skip_at % tn:
        tn //= 2
    per_b = seq // tm
    row_align = math.gcd(tn, skip)

    def w_rows(i, j):
        return pl.multiple_of(jnp.where(j < skip_at // tn, j * tn, j * tn + skip), row_align), 0
    return pl.pallas_call(
        functools.partial(_inproj_kernel, n_plain=n_plain_cols // tn),
        grid=(t // tm, n // tn),
        in_specs=[pl.BlockSpec((tm, d), lambda i, j: (i, 0)),
                  pl.BlockSpec((None, 1, d), lambda i, j: (i // per_b, 0, 0)),
                  pl.BlockSpec((None, 1, d), lambda i, j: (i // per_b, 0, 0)),
                  pl.BlockSpec((pl.Element(tn), pl.Element(d)), w_rows),
                  pl.BlockSpec((1, tn), lambda i, j: (0, j)),
                  pl.BlockSpec((LANES, d), lambda i, j: (0, 0))],
        out_specs=[pl.BlockSpec((tm, tn), lambda i, j: (i, j)),
                   pl.BlockSpec((tm, LANES), lambda i, j: (i, 0))],
        out_shape=[jax.ShapeDtypeStruct((t, n), BF16),
                   jax.ShapeDtypeStruct((t, LANES), F32)],
        scratch_shapes=[pltpu.VMEM((tm, d), BF16)],
        compiler_params=_cparams(("parallel", "arbitrary")),
        name="in_proj",
    )(x2, sc, sh, w_packed, bias_packed, wf)


def _cum_kernel(f_ref, b_ref, o_ref, carry_ref, *, per_b):
    i = pl.program_id(0)

    @pl.when(i % per_b == 0)
    def _():
        carry_ref[...] = jnp.zeros_like(carry_ref)

    z = f_ref[...] + b_ref[...]
    lf = jnp.minimum(z, 0.0) - jnp.log(1.0 + jnp.exp(-jnp.abs(z)))
    tb = lf.shape[0]
    row = lax.broadcasted_iota(I32, (tb, tb), 0)
    col = lax.broadcasted_iota(I32, (tb, tb), 1)
    tri = jnp.where(col <= row, 1.0, 0.0).astype(F32)
    cum = jnp.dot(tri, lf, preferred_element_type=F32,
                  precision=lax.Precision.HIGHEST) + carry_ref[...]
    o_ref[...] = cum
    carry_ref[...] = cum[tb - 1:tb, :]


def _forget_cumsum(fa, bf_pad, seq):
    t = fa.shape[0]
    tb = _pick(seq, 512)
    return pl.pallas_call(
        functools.partial(_cum_kernel, per_b=seq // tb),
        grid=(t // tb,),
        in_specs=[pl.BlockSpec((tb, LANES), lambda i: (i, 0)),
                  pl.BlockSpec((1, LANES), lambda i: (0, 0))],
        out_specs=pl.BlockSpec((tb, LANES), lambda i: (i, 0)),
        out_shape=jax.ShapeDtypeStruct((t, LANES), F32),
        scratch_shapes=[pltpu.VMEM((1, LANES), F32)],
        compiler_params=_cparams(("arbitrary",)),
        name="forget_cumsum",
    )(fa, bf_pad)


def _hs(h):
    return slice(h * HEAD_DIM, (h + 1) * HEAD_DIM)


def _fox_kernel(q_ref, k_ref, v_ref, ck_ref, o_ref, *, tq, tk, hp):
    i = pl.program_id(2)
    q0 = pl.multiple_of(i * tq, tq)
    heads = range(hp)
    n_diag = tq // tk
    qs = [q_ref[:, _hs(h)] for h in heads]
    c0s = [ck_ref[h, :, pl.ds(q0, tk)][:, 0:1] for h in heads]

    def scores(h, k0):
        s = _dot_nt(qs[h], k_ref[pl.ds(k0, tk), _hs(h)])
        return s + (c0s[h] - ck_ref[h, :, pl.ds(k0, tk)]) * LOG2E

    def update(h, s, k0, m, l, acc):
        m_new = jnp.maximum(m, jnp.max(s, axis=1, keepdims=True))
        p = jnp.exp2(s - m_new)
        alpha = jnp.exp2(m - m_new)
        l = alpha * l + jnp.sum(p, axis=1, keepdims=True)
        acc = alpha * acc + _dot(p.astype(BF16), v_ref[pl.ds(k0, tk), _hs(h)])
        return m_new, l, acc

    def body(j, carry):
        carry = list(carry)
        for u in range(n_diag):
            k0 = pl.multiple_of((j * n_diag + u) * tk, tk)
            for h in heads:
                carry[h] = update(h, scores(h, k0), k0, *carry[h])
        return tuple(carry)

    init = tuple((jnp.full((tq, 1), NEG_BIG, F32), jnp.zeros((tq, 1), F32),
                  jnp.zeros((tq, HEAD_DIM), F32)) for _ in heads)
    carry = list(lax.fori_loop(0, i, body, init))
    row = lax.broadcasted_iota(I32, (tq, tk), 0)
    col = lax.broadcasted_iota(I32, (tq, tk), 1)
    for dblk in range(n_diag):
        off = dblk * tk
        k0 = pl.multiple_of(q0 + off, tk)
        for h in heads:
            s = jnp.where(col + off <= row, scores(h, k0), NEG_BIG)
            carry[h] = update(h, s, k0, *carry[h])
    for h in heads:
        _, l, acc = carry[h]
        o_ref[:, _hs(h)] = (acc / l).astype(o_ref.dtype)


def _fox_attention(proj, cum_rows, bsz, seq, heads, q_blk, k_blk, v_blk):
    tq = _pick(seq, FOX_TQ)
    tk = _pick(tq, FOX_TK)
    hp = _pick(heads, FOX_HP)
    nq = seq // tq
    wd = hp * HEAD_DIM
    return pl.pallas_call(
        functools.partial(_fox_kernel, tq=tq, tk=tk, hp=hp),
        grid=(bsz, heads // hp, nq),
        in_specs=[pl.BlockSpec((tq, wd), lambda b, h, i: (b * nq + i, q_blk // hp + h)),
                  pl.BlockSpec((seq, wd), lambda b, h, i: (b, k_blk // hp + h)),
                  pl.BlockSpec((seq, wd), lambda b, h, i: (b, v_blk // hp + h)),
                  pl.BlockSpec((hp, 1, seq), lambda b, h, i: (b * (heads // hp) + h, 0, 0))],
        out_specs=pl.BlockSpec((tq, wd), lambda b, h, i: (b * nq + i, h)),
        out_shape=jax.ShapeDtypeStruct((bsz * seq, heads * HEAD_DIM), BF16),
        compiler_params=_cparams(("parallel", "parallel", "arbitrary")),
        name="fox_attention",
    )(proj, proj, proj, cum_rows)


def _sb_kernel(q_ref, k_ref, v_ref, o_ref, *, tq, tk, hp):
    i = pl.program_id(2)
    q0 = pl.multiple_of(i * tq, tq)
    heads = range(hp)
    n_diag = tq // tk
    r_io = lax.broadcasted_iota(I32, (tk, tk), 0)
    c_io = lax.broadcasted_iota(I32, (tk, tk), 1)
    not_before = jnp.where(r_io >= c_io, 1.0, 0.0).astype(BF16)
    row = lax.broadcasted_iota(I32, (tq, tk), 0)
    col = lax.broadcasted_iota(I32, (tq, tk), 1)
    qs = [q_ref[:, _hs(h)] for h in heads]

    def tile(h, k0):
        z = _dot_nt(qs[h], k_ref[pl.ds(k0, tk), _hs(h)])
        sp = jnp.where(z > SOFTPLUS_LINEAR, z, jnp.log2(1.0 + jnp.exp2(z)))
        return z, sp

    runs = [jnp.zeros((tq, 1), F32) for _ in heads]
    accs = [jnp.zeros((tq, HEAD_DIM), F32) for _ in heads]
    for dblk in range(n_diag):
        off = (n_diag - 1 - dblk) * tk
        valid = col + off < row
        k0 = pl.multiple_of(q0 + off, tk)
        for h in heads:
            z, sp = tile(h, k0)
            sp = jnp.where(valid, sp, 0.0)
            suffix = _dot(sp.astype(BF16), not_before)
            w = jnp.where(valid, jnp.exp2(z - suffix - runs[h]), 0.0)
            accs[h] = accs[h] + _dot(w.astype(BF16), v_ref[pl.ds(k0, tk), _hs(h)])
            runs[h] = runs[h] + suffix[:, 0:1]

    def body(n, carry):
        runs, accs = list(carry[0]), list(carry[1])
        for u in range(n_diag):
            k0 = pl.multiple_of(q0 - (n * n_diag + u + 1) * tk, tk)
            for h in heads:
                z, sp = tile(h, k0)
                suffix = _dot(sp.astype(BF16), not_before)
                w = jnp.exp2(z - suffix - runs[h])
                accs[h] = accs[h] + _dot(w.astype(BF16), v_ref[pl.ds(k0, tk), _hs(h)])
                runs[h] = runs[h] + suffix[:, 0:1]
        return tuple(runs), tuple(accs)

    _, accs = lax.fori_loop(0, i, body, (tuple(runs), tuple(accs)))
    for h in heads:
        o_ref[:, _hs(h)] = accs[h].astype(o_ref.dtype)


def _sb_attention(proj, bsz, seq, heads, q_blk, k_blk, v_blk):
    tq = _pick(seq, SB_TQ)
    tk = _pick(tq, SB_TK)
    hp = _pick(heads, SB_HP)
    nq = seq // tq
    wd = hp * HEAD_DIM
    return pl.pallas_call(
        functools.partial(_sb_kernel, tq=tq, tk=tk, hp=hp),
        grid=(bsz, heads // hp, nq),
        in_specs=[pl.BlockSpec((tq, wd), lambda b, h, i: (b * nq + i, q_blk // hp + h)),
                  pl.BlockSpec((seq, wd), lambda b, h, i: (b, k_blk // hp + h)),
                  pl.BlockSpec((seq, wd), lambda b, h, i: (b, v_blk // hp + h))],
        out_specs=pl.BlockSpec((tq, wd), lambda b, h, i: (b * nq + i, h)),
        out_shape=jax.ShapeDtypeStruct((bsz * seq, heads * HEAD_DIM), BF16),
        compiler_params=_cparams(("parallel", "parallel", "arbitrary")),
        name="sb_attention",
    )(proj, proj, proj)


def _merge_kernel(of_ref, os_ref, wa_ref, wb_ref, gf_ref, gs_ref, o_ref):
    yf = _dot(of_ref[...], wa_ref[...])
    ys = _dot(os_ref[...], wb_ref[...])
    o_ref[...] = (gf_ref[...].astype(F32) * yf + gs_ref[...].astype(F32) * ys).astype(o_ref.dtype)


def _merge(o_fox, o_sb, wa, wb, proj, g_col0, d):
    t = o_fox.shape[0]
    tm = _pick(t, 512)
    tn = _pick(d, 1024)
    gf0 = g_col0 // tn
    gs0 = (g_col0 + d) // tn
    return pl.pallas_call(
        _merge_kernel,
        grid=(t // tm, d // tn),
        in_specs=[pl.BlockSpec((tm, o_fox.shape[1]), lambda i, j: (i, 0)),
                  pl.BlockSpec((tm, o_sb.shape[1]), lambda i, j: (i, 0)),
                  pl.BlockSpec((wa.shape[0], tn), lambda i, j: (0, j)),
                  pl.BlockSpec((wb.shape[0], tn), lambda i, j: (0, j)),
                  pl.BlockSpec((tm, tn), lambda i, j: (i, gf0 + j)),
                  pl.BlockSpec((tm, tn), lambda i, j: (i, gs0 + j))],
        out_specs=pl.BlockSpec((tm, tn), lambda i, j: (i, j)),
        out_shape=jax.ShapeDtypeStruct((t, d), BF16),
        compiler_params=_cparams(("parallel", "arbitrary")),
        name="branch_merge",
    )(o_fox, o_sb, wa, wb, proj, proj)


def _layer_norm(v, g, b):
    mu = jnp.mean(v, axis=-1, keepdims=True)
    c = v - mu
    var = jnp.mean(c * c, axis=-1, keepdims=True)
    return c * lax.rsqrt(var + LN_EPS) * g + b


def _pack_halves(a):
    half = a.shape[1] // 2
    lo = lax.bitcast_convert_type(a[:, :half].astype(BF16).astype(F32), jnp.uint32)
    hi = lax.bitcast_convert_type(a[:, half:].astype(BF16).astype(F32), jnp.uint32)
    return (hi & jnp.uint32(0xFFFF0000)) | (lo >> 16)


def _unpack_halves(p):
    lo = lax.bitcast_convert_type(p << 16, F32)
    hi = lax.bitcast_convert_type(p & jnp.uint32(0xFFFF0000), F32)
    return lo, hi


def _matmul_kernel(a_ref, w_ref, o_ref):
    o_ref[...] = _dot(a_ref[...], w_ref[...]).astype(o_ref.dtype)


def _out_proj(merged, w_out):
    t, d = merged.shape
    n = w_out.shape[1]
    tm = _pick(t, 1024)
    tn = _pick(n, 1024)
    return pl.pallas_call(
        _matmul_kernel,
        grid=(t // tm, n // tn),
        in_specs=[pl.BlockSpec((tm, d), lambda i, j: (i, 0)),
                  pl.BlockSpec((d, tn), lambda i, j: (0, j))],
        out_specs=pl.BlockSpec((tm, tn), lambda i, j: (i, j)),
        out_shape=jax.ShapeDtypeStruct((t, n), BF16),
        compiler_params=_cparams(("parallel", "arbitrary")),
        name="out_proj",
    )(merged, w_out)


def _ln1_kernel(y_ref, x_ref, g1_ref, lg_ref, lb_ref, sc_ref, sh_ref, x1_ref, h2_ref, *, alpha):
    v = alpha * x_ref[...] + g1_ref[...] * y_ref[...].astype(F32)
    x1 = _layer_norm(v, lg_ref[...], lb_ref[...])
    x1_ref[...] = x1
    h2_ref[...] = _pack_halves(x1 * (1.0 + sc_ref[...]) + sh_ref[...])


def _ln1(mix, x2, g1, ln_g, ln_b, sc2, sh2, seq, alpha):
    t, d = x2.shape
    tm = _pick(seq, 256)
    per_b = seq // tm
    row = lambda i: (i, 0)
    per_batch = lambda i: (i // per_b, 0, 0)
    const = lambda i: (0, 0)
    return pl.pallas_call(
        functools.partial(_ln1_kernel, alpha=alpha),
        grid=(t // tm,),
        in_specs=[pl.BlockSpec((tm, d), row),
                  pl.BlockSpec((tm, d), row),
                  pl.BlockSpec((None, 1, d), per_batch),
                  pl.BlockSpec((1, d), const),
                  pl.BlockSpec((1, d), const),
                  pl.BlockSpec((None, 1, d), per_batch),
                  pl.BlockSpec((None, 1, d), per_batch)],
        out_specs=[pl.BlockSpec((tm, d), row), pl.BlockSpec((tm, d // 2), row)],
        out_shape=[jax.ShapeDtypeStruct((t, d), F32), jax.ShapeDtypeStruct((t, d // 2), jnp.uint32)],
        compiler_params=_cparams(("parallel",)),
        name="residual_ln1",
    )(mix, x2, g1, ln_g, ln_b, sc2, sh2)


def _route_kernel(x1_ref, sc_ref, sh_ref, wr_ref, rb_ref, idx_ref, rank_ref, w_ref, cnt_ref, carry_ref,
                  *, n_exp):
    i = pl.program_id(0)

    @pl.when(i == 0)
    def _():
        carry_ref[...] = jnp.zeros_like(carry_ref)

    tb = x1_ref.shape[0]
    gsz = n_exp // N_GROUPS
    h = x1_ref[...] * (1.0 + sc_ref[...]) + sh_ref[...]
    logits = _dot_nt(wr_ref[...], h, precision=lax.Precision.HIGHEST)
    scores = 1.0 / (1.0 + jnp.exp(-logits))
    choice = scores + rb_ref[...][:, 0:1]
    neg = -jnp.inf

    g = choice.reshape(N_GROUPS, gsz, tb)
    io = lax.broadcasted_iota(I32, g.shape, 1)
    m1 = jnp.max(g, axis=1, keepdims=True)
    first = jnp.min(jnp.where(g == m1, io, gsz), axis=1, keepdims=True)
    m2 = jnp.max(jnp.where(io == first, neg, g), axis=1, keepdims=True)
    gs = (m1 + m2).reshape(N_GROUPS, tb)

    gio = lax.broadcasted_iota(I32, gs.shape, 0)
    gsel = jnp.zeros(gs.shape, F32)
    for _ in range(TOPK_GROUPS):
        m = jnp.max(gs, axis=0, keepdims=True)
        f = jnp.min(jnp.where(gs == m, gio, N_GROUPS), axis=0, keepdims=True)
        hit = gio == f
        gsel = jnp.where(hit, 1.0, gsel)
        gs = jnp.where(hit, neg, gs)
    emask = jnp.broadcast_to(gsel.reshape(N_GROUPS, 1, tb), (N_GROUPS, gsz, tb)).reshape(n_exp, tb)

    v = jnp.where(emask > 0.5, choice, neg)
    eio = lax.broadcasted_iota(I32, v.shape, 0)
    sel = jnp.zeros(v.shape, F32)
    idx_rows, w_rows = [], []
    for _ in range(TOP_K):
        m = jnp.max(v, axis=0, keepdims=True)
        f = jnp.min(jnp.where(v == m, eio, n_exp), axis=0, keepdims=True)
        hit = eio == f
        idx_rows.append(f)
        w_rows.append(jnp.sum(jnp.where(hit, scores, 0.0), axis=0, keepdims=True))
        sel = jnp.where(hit, 1.0, sel)
        v = jnp.where(hit, neg, v)
    wsum = w_rows[0]
    for r in range(1, TOP_K):
        wsum = wsum + w_rows[r]
    inv = 1.0 / (wsum + 1e-20)

    r_io = lax.broadcasted_iota(I32, (tb, tb), 0)
    c_io = lax.broadcasted_iota(I32, (tb, tb), 1)
    before = jnp.where(r_io < c_io, 1.0, 0.0).astype(BF16)
    excl = _dot(sel.astype(BF16), before) + carry_ref[...][:, 0:1]
    for r in range(TOP_K):
        hit = eio == idx_rows[r]
        rank = jnp.sum(jnp.where(hit, excl, 0.0), axis=0, keepdims=True)
        idx_ref[r:r + 1, :] = idx_rows[r]
        rank_ref[r:r + 1, :] = rank.astype(I32)
        w_ref[r:r + 1, :] = w_rows[r] * inv * ROUTED_SCALE
    total = carry_ref[...] + jnp.sum(sel, axis=1, keepdims=True)
    carry_ref[...] = total
    cnt_ref[...] = total.astype(I32)


def _route(x1, sc2, sh2, wr_t, rb, seq):
    t, d = x1.shape
    n_exp = wr_t.shape[0]
    tb = _pick(seq, 512)
    per_b = seq // tb
    blk = pl.BlockSpec((TOP_K, tb), lambda i: (0, i))
    return pl.pallas_call(
        functools.partial(_route_kernel, n_exp=n_exp),
        grid=(t // tb,),
        in_specs=[pl.BlockSpec((tb, d), lambda i: (i, 0)),
                  pl.BlockSpec((None, 1, d), lambda i: (i // per_b, 0, 0)),
                  pl.BlockSpec((None, 1, d), lambda i: (i // per_b, 0, 0)),
                  pl.BlockSpec((n_exp, d), lambda i: (0, 0)),
                  pl.BlockSpec((n_exp, LANES), lambda i: (0, 0))],
        out_specs=[blk, blk, blk, pl.BlockSpec((n_exp, LANES), lambda i: (0, 0))],
        out_shape=[jax.ShapeDtypeStruct((TOP_K, t), I32),
                   jax.ShapeDtypeStruct((TOP_K, t), I32),
                   jax.ShapeDtypeStruct((TOP_K, t), F32),
                   jax.ShapeDtypeStruct((n_exp, LANES), I32)],
        scratch_shapes=[pltpu.VMEM((n_exp, LANES), F32)],
        compiler_params=_cparams(("arbitrary",)),
        name="router_topk",
    )(x1, sc2, sh2, wr_t, rb)


def _dest_kernel(idx_ref, rank_ref, cnt_ref, dest_ref, be_ref, meta_ref, *, n_exp, tm, nbp):
    cnt = cnt_ref[...]
    nblk = jnp.right_shift(cnt + (tm - 1), tm.bit_length() - 1)
    r_io = lax.broadcasted_iota(I32, (n_exp, n_exp), 0)
    c_io = lax.broadcasted_iota(I32, (n_exp, n_exp), 1)
    below = jnp.where(c_io < r_io, 1.0, 0.0).astype(F32)
    start_blk = jnp.dot(below, nblk.astype(F32), preferred_element_type=F32,
                        precision=lax.Precision.HIGHEST).astype(I32)
    end_blk = start_blk + nblk
    start_row = start_blk * tm
    idx = idx_ref[...]
    dest = rank_ref[...]
    for e in range(n_exp):
        dest = dest + jnp.where(idx == e, start_row[e:e + 1, 0:1], 0)
    dest_ref[...] = dest
    b_io = lax.broadcasted_iota(I32, (n_exp, nbp), 1)
    done = jnp.sum(jnp.where(b_io >= end_blk[:, 0:1], 1.0, 0.0), axis=0, keepdims=True)
    be_ref[...] = jnp.minimum(done.astype(I32), n_exp - 1)
    meta_ref[...] = end_blk[n_exp - 1:n_exp, :]


def _dest(idx, rank, counts, n_exp, tm, n_blocks):
    assert tm & (tm - 1) == 0
    t = idx.shape[1]
    tb = _pick(t, 2048)
    nbp = -(-n_blocks // LANES) * LANES
    blk = pl.BlockSpec((TOP_K, tb), lambda i: (0, i))
    return pl.pallas_call(
        functools.partial(_dest_kernel, n_exp=n_exp, tm=tm, nbp=nbp),
        grid=(t // tb,),
        in_specs=[blk, blk, pl.BlockSpec((n_exp, LANES), lambda i: (0, 0))],
        out_specs=[blk, pl.BlockSpec((1, nbp), lambda i: (0, 0)),
                   pl.BlockSpec((1, LANES), lambda i: (0, 0))],
        out_shape=[jax.ShapeDtypeStruct((TOP_K, t), I32),
                   jax.ShapeDtypeStruct((1, nbp), I32),
                   jax.ShapeDtypeStruct((1, LANES), I32)],
        compiler_params=_cparams(("arbitrary",)),
        name="dispatch_rows",
    )(idx, rank, counts)


def _invert_kernel(dest_ref, tok_ref, *, chunk, n_tok, n_rows):
    c = pl.program_id(0)

    @pl.when(c == 0)
    def _():
        def zero(r, _):
            tok_ref[r] = 0
            return 0
        lax.fori_loop(0, n_rows, zero, 0, unroll=DMA_UNROLL)

    tok0 = (c * chunk) % n_tok

    def place(n, _):
        tok_ref[dest_ref[n]] = tok0 + n
        return 0
    lax.fori_loop(0, chunk, place, 0, unroll=DMA_UNROLL)


def _invert(dest_flat, n_tok, n_rows):
    n_assign = dest_flat.shape[0]
    chunk = _pick(n_tok, 8192)
    return pl.pallas_call(
        functools.partial(_invert_kernel, chunk=chunk, n_tok=n_tok, n_rows=n_rows),
        grid=(n_assign // chunk,),
        in_specs=[pl.BlockSpec((chunk,), lambda c: (c,), memory_space=pltpu.SMEM)],
        out_specs=pl.BlockSpec(memory_space=pltpu.SMEM),
        out_shape=jax.ShapeDtypeStruct((n_rows,), I32),
        compiler_params=_cparams(("arbitrary",)),
        name="dispatch_tokens",
    )(dest_flat)


def _swiglu_packed(xp, wg_ref, wu_ref, wd_ref):
    lo, hi = _unpack_halves(xp)
    x = jnp.concatenate([lo.astype(BF16), hi.astype(BF16)], axis=1)
    g = _dot(x, wg_ref[...])
    u = _dot(x, wu_ref[...])
    a = (g * (1.0 / (1.0 + jnp.exp(-g))) * u).astype(BF16)
    return _dot(a, wd_ref[...])


def _moe_kernel(tok_ref, be_ref, meta_ref, h_hbm, wg_ref, wu_ref, wd_ref, y_ref, xbuf, xp_ref, sem,
                *, tm):
    b = pl.program_id(0)
    used = meta_ref[0]

    def row_copy(blk, slot, r):
        tok = tok_ref[blk * tm + r]
        return pltpu.make_async_copy(h_hbm.at[pl.ds(tok, 1)], xbuf.at[slot, pl.ds(r, 1)],
                                     sem.at[slot])

    def gather(blk, slot):
        def issue(r, _):
            row_copy(blk, slot, r).start()
            return 0
        lax.fori_loop(0, tm, issue, 0, unroll=DMA_UNROLL)

    @pl.when(b == 0)
    def _():
        gather(0, 0)
        gather(1, 1)

    @pl.when(b <= used + 1)
    def _():
        slot = b % MOE_SLOTS
        pltpu.make_async_copy(h_hbm.at[pl.ds(0, tm)], xbuf.at[slot], sem.at[slot]).wait()

    @pl.when(b < used)
    def _():
        xp_ref[...] = xbuf[b % MOE_SLOTS]
        nxt = (b + 2) % MOE_SLOTS
        for r in range(tm):
            row_copy(b + 2, nxt, r).start()
        y_ref[...] = _pack_halves(_swiglu_packed(xp_ref[...], wg_ref, wu_ref, wd_ref))

    @pl.when(b >= used)
    def _():
        y_ref[...] = jnp.zeros_like(y_ref)


def _moe(tok_buf, block_e, meta, h2p, wg, wu, wd, tm):
    n_rows = tok_buf.shape[0]
    dp = h2p.shape[1]
    d = 2 * dp
    f = wg.shape[2]
    grid_spec = pltpu.PrefetchScalarGridSpec(
        num_scalar_prefetch=3,
        grid=(n_rows // tm,),
        in_specs=[pl.BlockSpec(memory_space=pl.ANY),
                  pl.BlockSpec((None, d, f), lambda b, tok, be, meta: (be[b], 0, 0)),
                  pl.BlockSpec((None, d, f), lambda b, tok, be, meta: (be[b], 0, 0)),
                  pl.BlockSpec((None, f, d), lambda b, tok, be, meta: (be[b], 0, 0))],
        out_specs=pl.BlockSpec((tm, dp), lambda b, tok, be, meta: (b, 0)),
        scratch_shapes=[pltpu.VMEM((MOE_SLOTS, tm, dp), jnp.uint32), pltpu.VMEM((tm, dp), jnp.uint32),
                        pltpu.SemaphoreType.DMA((MOE_SLOTS,))],
    )
    return pl.pallas_call(
        functools.partial(_moe_kernel, tm=tm),
        grid_spec=grid_spec,
        out_shape=jax.ShapeDtypeStruct((n_rows, dp), jnp.uint32),
        compiler_params=_cparams(("arbitrary",), disable_bounds_checks=True),
        name="routed_experts",
    )(tok_buf, block_e, meta, h2p, wg, wu, wd)


def _shared_kernel(h_ref, wg_ref, wu_ref, wd_ref, o_ref):
    o_ref[...] = _swiglu_packed(h_ref[...], wg_ref, wu_ref, wd_ref).astype(o_ref.dtype)


def _shared(h2p, wg, wu, wd):
    t, dp = h2p.shape
    d = 2 * dp
    f = wg.shape[1]
    tm = _pick(t, 512)
    return pl.pallas_call(
        _shared_kernel,
        grid=(t // tm,),
        in_specs=[pl.BlockSpec((tm, dp), lambda i: (i, 0)),
                  pl.BlockSpec((d, f), lambda i: (0, 0)),
                  pl.BlockSpec((d, f), lambda i: (0, 0)),
                  pl.BlockSpec((f, d), lambda i: (0, 0))],
        out_specs=pl.BlockSpec((tm, d), lambda i: (i, 0)),
        out_shape=jax.ShapeDtypeStruct((t, d), BF16),
        compiler_params=_cparams(("parallel",)),
        name="shared_expert",
    )(h2p, wg, wu, wd)


def _final_kernel(dest_ref, y_hbm, w_ref, x1_ref, s_ref, g2_ref, lg_ref, lb_ref, o_ref, ybuf, sem,
                  *, tb, n_tok, alpha, last_slot):
    i = pl.program_id(0)
    last = pl.num_programs(0) - 1

    def row_copy(blk, slot, r, n):
        row = dest_ref[r * n_tok + blk * tb + n]
        return pltpu.make_async_copy(y_hbm.at[pl.ds(row, 1)], ybuf.at[slot, pl.ds(r * tb + n, 1)],
                                     sem.at[slot])

    @pl.when(i == 0)
    def _():
        for r in range(TOP_K):
            def issue(n, _):
                row_copy(0, 0, r, n).start()
                return 0
            lax.fori_loop(0, tb, issue, 0, unroll=DMA_UNROLL)

    def step(slot, prefetch):
        pltpu.make_async_copy(y_hbm.at[pl.ds(0, TOP_K * tb)], ybuf.at[slot], sem.at[slot]).wait()
        if prefetch:
            for r in range(TOP_K):
                for n in range(tb):
                    row_copy(i + 1, 1 - slot, r, n).start()
        w = w_ref[...]
        lo, hi = _unpack_halves(ybuf[slot, 0:tb])
        r_lo, r_hi = w[:, 0:1] * lo, w[:, 0:1] * hi
        for r in range(1, TOP_K):
            lo, hi = _unpack_halves(ybuf[slot, r * tb:(r + 1) * tb])
            r_lo, r_hi = r_lo + w[:, r:r + 1] * lo, r_hi + w[:, r:r + 1] * hi
        ffn = jnp.concatenate([r_lo, r_hi], axis=1) + s_ref[...].astype(F32)
        v = alpha * x1_ref[...] + g2_ref[...] * ffn
        o_ref[...] = _layer_norm(v, lg_ref[...], lb_ref[...])

    for slot in range(2):
        @pl.when((i % 2 == slot) & (i < last))
        def _():
            step(slot, True)

    @pl.when(i == last)
    def _():
        step(last_slot, False)


def _final(dest, ys, w_tok, x1, shared, g2, ln_g, ln_b, seq, alpha):
    t, d = x1.shape
    tb = _pick(seq, COMBINE_TB)
    per_b = seq // tb
    grid_spec = pltpu.PrefetchScalarGridSpec(
        num_scalar_prefetch=1,
        grid=(t // tb,),
        in_specs=[pl.BlockSpec(memory_space=pl.ANY),
                  pl.BlockSpec((tb, TOP_K), lambda i, dest: (i, 0)),
                  pl.BlockSpec((tb, d), lambda i, dest: (i, 0)),
                  pl.BlockSpec((tb, d), lambda i, dest: (i, 0)),
                  pl.BlockSpec((None, 1, d), lambda i, dest: (i // per_b, 0, 0)),
                  pl.BlockSpec((1, d), lambda i, dest: (0, 0)),
                  pl.BlockSpec((1, d), lambda i, dest: (0, 0))],
        out_specs=pl.BlockSpec((tb, d), lambda i, dest: (i, 0)),
        scratch_shapes=[pltpu.VMEM((2, TOP_K * tb, d // 2), jnp.uint32),
                        pltpu.SemaphoreType.DMA((2,))],
    )
    return pl.pallas_call(
        functools.partial(_final_kernel, tb=tb, n_tok=t, alpha=alpha, last_slot=(t // tb - 1) % 2),
        grid_spec=grid_spec,
        out_shape=jax.ShapeDtypeStruct((t, d), F32),
        compiler_params=_cparams(("arbitrary",), disable_bounds_checks=True),
        name="combine_ln2",
    )(dest, ys, w_tok, x1, shared, g2, ln_g, ln_b)


def _layer(x2, c, bsz, seq, w_ada, b_ada, w_in, b_forget, b_gate, w_branch_fox, w_branch_sb, w_out,
           ln1_g, ln1_b, w_router, router_bias, w_exp_gate, w_exp_up, w_exp_down,
           w_sh_gate, w_sh_up, w_sh_down, ln2_g, ln2_b, alpha):
    t, d = x2.shape
    hf = b_forget.shape[0]
    fw = w_branch_fox.shape[0]
    sw = w_branch_sb.shape[0]
    hs = sw // HEAD_DIM
    n_exp = w_router.shape[1]

    mod = _ada(c, w_ada, b_ada)
    sh1, sc1, g1, sh2, sc2, g2 = [m.reshape(bsz, 1, d) for m in jnp.split(mod, 6, axis=-1)]

    f0 = 3 * fw
    qkv_cols = 3 * fw + 3 * sw
    qs = HEAD_DIM ** -0.5 * LOG2E
    wt = w_in.T
    col_scale = jnp.concatenate([jnp.full((fw,), qs, F32), jnp.ones((2 * fw + hf,), F32),
                                 jnp.full((sw,), qs, F32), jnp.ones((2 * sw + 2 * d,), F32)])
    w_packed = (wt * col_scale[:, None]).astype(BF16)
    bias_packed = jnp.concatenate([jnp.zeros((qkv_cols,), F32), b_gate]).reshape(1, -1)
    wf = jnp.zeros((LANES, d), BF16).at[:hf].set(w_packed[f0:f0 + hf])
    bf_pad = jnp.zeros((1, LANES), F32).at[0, :hf].set(b_forget)

    proj, fa = _inproj(x2, sc1, sh1, w_packed, bias_packed, wf, seq, qkv_cols, f0, hf)
    cum = _forget_cumsum(fa, bf_pad, seq)
    cum_rows = cum[:, :hf].reshape(bsz, seq, hf).transpose(0, 2, 1).reshape(bsz * hf, 1, seq)

    o_fox = _fox_attention(proj, cum_rows, bsz, seq, hf, 0, hf, 2 * hf)
    o_sb = _sb_attention(proj, bsz, seq, hs, 3 * hf, 3 * hf + hs, 3 * hf + 2 * hs)
    merged = _merge(o_fox, o_sb, w_branch_fox.astype(BF16), w_branch_sb.astype(BF16), proj,
                    qkv_cols, d)
    mix = _out_proj(merged, w_out.astype(BF16))
    x1, h2 = _ln1(mix, x2, g1, ln1_g.reshape(1, d), ln1_b.reshape(1, d), sc2, sh2, seq, alpha)

    rb = jnp.broadcast_to(router_bias.reshape(n_exp, 1), (n_exp, LANES))
    idx, rank, w_top, counts = _route(x1, sc2, sh2, w_router.T, rb, seq)
    tm = MOE_TM
    n_rows = t * TOP_K + (n_exp + 1) * tm
    dest2, be2, meta2 = _dest(idx, rank, counts, n_exp, tm, n_rows // tm)
    dest, block_e, meta = dest2.reshape(-1), be2[0, :n_rows // tm], meta2[0, :1]
    tok_buf = _invert(dest, t, n_rows)
    ys = _moe(tok_buf, block_e, meta, h2, w_exp_gate.astype(BF16), w_exp_up.astype(BF16),
              w_exp_down.astype(BF16), tm)
    shared = _shared(h2, w_sh_gate.astype(BF16), w_sh_up.astype(BF16), w_sh_down.astype(BF16))
    return _final(dest, ys, w_top.T, x1, shared, g2, ln2_g.reshape(1, d), ln2_b.reshape(1, d),
                  seq, alpha)


def kernel(x, c, w_ada, b_ada, w_in, b_forget, b_gate, w_branch_fox, w_branch_sb, w_out, ln1_g, ln1_b, w_router, router_bias, w_exp_gate, w_exp_up, w_exp_down, w_sh_gate, w_sh_up, w_sh_down, ln2_g, ln2_b):
    bsz, seq, d = x.shape
    depth = w_ada.shape[0]
    alpha = (2 * depth) ** 0.25
    x2 = x.reshape(bsz * seq, d)
    for l in range(depth):
        x2 = _layer(x2, c, bsz, seq, w_ada[l], b_ada[l], w_in[l], b_forget[l], b_gate[l],
                    w_branch_fox[l], w_branch_sb[l], w_out[l], ln1_g[l], ln1_b[l], w_router[l],
                    router_bias[l], w_exp_gate[l], w_exp_up[l], w_exp_down[l], w_sh_gate[l],
                    w_sh_up[l], w_sh_down[l], ln2_g[l], ln2_b[l], alpha)
    return x2.reshape(bsz, seq, d)
```
